```python
import math
import jax, jax.numpy as jnp
from jax import lax
import numpy as np

D_MODEL = 4096
BATCH = 4
SEQ = 2048
DEPTH = 1
DEC_BATCH = 128
DEC_SEQ = 4
PAST_LEN = 16384
PAGE_SIZE = 128

GLA_HEADS = 16
GLA_KEY_W = D_MODEL // 2
GLA_VAL_W = D_MODEL
GLA_DK = GLA_KEY_W // GLA_HEADS
GLA_DV = GLA_VAL_W // GLA_HEADS
GLA_GATE_RANK = 16
GLA_GATE_TAU = 16.0
GLA_CHUNK = 64
SSD_INNER = D_MODEL
SSD_HEADDIM = 64
SSD_HEADS = SSD_INNER // SSD_HEADDIM
SSD_GROUPS = 8
SSD_DSTATE = 128
SSD_CONV_W = 4
SSD_CONV_DIM = SSD_INNER + 2 * SSD_GROUPS * SSD_DSTATE
SSD_CHUNK = 64
MEM_LEN = 256
MEM_HEADS = 4
MEM_HD = 128
MEM_W = MEM_HEADS * MEM_HD
N_EXPERTS = 32
TOP_K = 4
EXPERT_FF = D_MODEL
SWIGLU_LIMIT = 7.0
SWIGLU_ALPHA = 1.702
MOE_BLOCK = 256
RMS_EPS = 1e-6

IN_WIDTHS = (GLA_KEY_W, GLA_KEY_W, GLA_VAL_W, GLA_VAL_W, GLA_GATE_RANK,
             SSD_INNER, SSD_CONV_DIM, SSD_HEADS, D_MODEL, D_MODEL)
IN_SPLITS = tuple(int(s) for s in np.cumsum(IN_WIDTHS)[:-1])
D_IN = int(sum(IN_WIDTHS))

kernel_name = 'hybrid_gla_ssd_memxattn_moe_step'


def _rmsnorm(x, g):
    xf = x.astype(jnp.float32)
    y = xf * lax.rsqrt(jnp.mean(xf * xf, axis=-1, keepdims=True) + RMS_EPS)
    return (y * g.astype(jnp.float32)).astype(x.dtype)


def _to_chunks(t, n, c):
    return jnp.moveaxis(t.reshape(t.shape[0], n, c, *t.shape[2:]), 1, 0)


def _from_chunks(t):
    t = jnp.moveaxis(t, 0, 1)
    return t.reshape(t.shape[0], t.shape[1] * t.shape[2], *t.shape[3:])


def _gla(q, k, v, log_a, s0):
    T = q.shape[1]
    c = math.gcd(T, GLA_CHUNK)
    n = T // c
    mask = jnp.tril(jnp.ones((c, c), bool))

    def step(S, inp):
        qc, kc, vc, lac = inp
        b = jnp.cumsum(lac, axis=1)
        qe = qc * jnp.exp(b)
        ke = kc * jnp.exp(-b)
        att = jnp.where(mask, jnp.einsum('bthk,bshk->bhts', qe, ke), 0.0)
        o = jnp.einsum('bhts,bshv->bthv', att, vc) + jnp.einsum('bthk,bhkv->bthv', qe, S)
        b_last = b[:, -1]
        S = jnp.exp(b_last)[..., None] * S + jnp.einsum(
            'bshk,bshv->bhkv', kc * jnp.exp(b_last[:, None] - b), vc)
        return S, o

    xs = tuple(_to_chunks(t.astype(jnp.float32), n, c) for t in (q, k, v, log_a))
    S, o = lax.scan(step, s0.astype(jnp.float32), xs)
    return _from_chunks(o), S


def _ssd(x, dt, a_neg, bm, cm, h0):
    bsz, T, H, P = x.shape
    G, N = bm.shape[2], bm.shape[3]
    R = H // G
    c = math.gcd(T, SSD_CHUNK)
    n = T // c
    mask = jnp.tril(jnp.ones((c, c), bool))[None, :, :, None, None]
    a_gr = a_neg.reshape(G, R)

    def step(h, inp):
        xc, dtc, bc, cc = inp
        cum = jnp.cumsum(dtc * a_gr, axis=1)
        seg = cum[:, :, None] - cum[:, None, :]
        decay = jnp.exp(jnp.where(mask, seg, -jnp.inf))
        cb = jnp.einsum('btgn,bsgn->btsg', cc, bc)
        w = cb[..., None] * decay * dtc[:, None]
        y = jnp.einsum('btsgr,bsgrp->btgrp', w, xc)
        y = y + jnp.einsum('btgn,bgrpn->btgrp', cc, h) * jnp.exp(cum)[..., None]
        last = cum[:, -1]
        wk = jnp.exp(last[:, None] - cum) * dtc
        h = jnp.exp(last)[..., None, None] * h + jnp.einsum('bsgn,bsgr,bsgrp->bgrpn', bc, wk, xc)
        return h, y

    f32 = jnp.float32
    xs = (_to_chunks(x.astype(f32).reshape(bsz, T, G, R, P), n, c),
          _to_chunks(dt.astype(f32).reshape(bsz, T, G, R), n, c),
          _to_chunks(bm.astype(f32), n, c), _to_chunks(cm.astype(f32), n, c))
    h, y = lax.scan(step, h0.astype(f32).reshape(bsz, G, R, P, N), xs)
    return _from_chunks(y).reshape(bsz, T, H, P), h.reshape(bsz, H, P, N)


def _causal_conv(xbc, state, w, b):
    T = xbc.shape[1]
    full = jnp.concatenate([state.astype(xbc.dtype), xbc], axis=1)
    out = b + full[:, 0:T] * w[0]
    for j in range(1, SSD_CONV_W):
        out = out + full[:, j:j + T] * w[j]
    return jax.nn.silu(out), full[:, T:]


def _mixer(x, s_gla, s_ssd, s_conv, lw):
    bsz, T, _ = x.shape
    f32 = jnp.float32
    h = _rmsnorm(x, lw['norm_mix'])
    proj = h @ lw['w_in']
    q, k, v, r, lr, z, xbc, dt_raw, ga, gb = jnp.split(proj, IN_SPLITS, axis=-1)
    qh = q.reshape(bsz, T, GLA_HEADS, GLA_DK) * (GLA_DK ** -0.5)
    kh = k.reshape(bsz, T, GLA_HEADS, GLA_DK)
    vh = v.reshape(bsz, T, GLA_HEADS, GLA_DV)
    zg = (lr @ lw['gla_gate_w'] + lw['gla_gate_b']).astype(f32)
    log_a = (jax.nn.log_sigmoid(zg) / GLA_GATE_TAU).reshape(bsz, T, GLA_HEADS, GLA_DK)
    o, s_gla_new = _gla(qh, kh, vh, log_a, s_gla)
    o = _rmsnorm(o.astype(x.dtype), lw['gla_norm']) * jax.nn.silu(r.reshape(bsz, T, GLA_HEADS, GLA_DV))
    o = o.reshape(bsz, T, GLA_VAL_W)
    xbc, s_conv_new = _causal_conv(xbc, s_conv, lw['ssd_conv_w'], lw['ssd_conv_b'])
    xs, bm, cm = jnp.split(xbc, (SSD_INNER, SSD_INNER + SSD_GROUPS * SSD_DSTATE), axis=-1)
    xh = xs.reshape(bsz, T, SSD_HEADS, SSD_HEADDIM)
    dt = jax.nn.softplus(dt_raw.astype(f32) + lw['ssd_dt_bias'].astype(f32))
    a_neg = -jnp.exp(lw['ssd_a_log'].astype(f32))
    y, s_ssd_new = _ssd(xh, dt, a_neg, bm.reshape(bsz, T, SSD_GROUPS, SSD_DSTATE),
                        cm.reshape(bsz, T, SSD_GROUPS, SSD_DSTATE), s_ssd)
    y = y + lw['ssd_d'].astype(f32)[:, None] * xh.astype(f32)
    y = y.astype(x.dtype).reshape(bsz, T, SSD_INNER) * jax.nn.silu(z)
    y = _rmsnorm(y.reshape(bsz, T, SSD_GROUPS, SSD_INNER // SSD_GROUPS),
                 lw['ssd_norm'].reshape(SSD_GROUPS, SSD_INNER // SSD_GROUPS)).reshape(bsz, T, SSD_INNER)
    merged = jax.nn.sigmoid(ga) * (o @ lw['w_proj_gla']) + jax.nn.sigmoid(gb) * (y @ lw['w_proj_ssd'])
    return (x + merged @ lw['w_out'], s_gla_new.astype(s_gla.dtype),
            s_ssd_new.astype(s_ssd.dtype), s_conv_new)


def _memory_kv(mem, g, w_ck, w_cv):
    bsz, n = mem.shape[:2]
    m = _rmsnorm(mem, g)
    return ((m @ w_ck).reshape(bsz, n, MEM_HEADS, MEM_HD),
            (m @ w_cv).reshape(bsz, n, MEM_HEADS, MEM_HD))


def _cross(x, mem_k, mem_v, lw):
    bsz, T, _ = x.shape
    h = _rmsnorm(x, lw['norm_cross'])
    q = (h @ lw['w_cq']).reshape(bsz, T, MEM_HEADS, MEM_HD)
    s = jnp.einsum('bthd,bmhd->bhtm', q, mem_k.astype(q.dtype)).astype(jnp.float32) * (MEM_HD ** -0.5)
    p = jax.nn.softmax(s, axis=-1).astype(x.dtype)
    o = jnp.einsum('bhtm,bmhd->bthd', p, mem_v.astype(x.dtype)).reshape(bsz, T, MEM_W)
    return x + o @ lw['w_co']


def _routed_experts(h, router_w, router_b, w1, b1, w2, b2):
    n_tok = h.shape[0]
    logits = (h @ router_w + router_b).astype(jnp.float32)
    top_v, top_i = lax.top_k(logits, TOP_K)
    gate = jax.nn.softmax(top_v, axis=-1)
    m = n_tok * TOP_K
    e_flat = top_i.reshape(-1).astype(jnp.int32)
    tok_flat = jnp.arange(m, dtype=jnp.int32) // TOP_K
    order = jnp.argsort(e_flat)
    e_s, tok_s, g_s = e_flat[order], tok_flat[order], gate.reshape(-1)[order]
    counts = jnp.zeros((N_EXPERTS,), jnp.int32).at[e_flat].add(1)
    start = jnp.cumsum(counts) - counts
    pad_counts = (counts + MOE_BLOCK - 1) // MOE_BLOCK * MOE_BLOCK
    pad_end = jnp.cumsum(pad_counts)
    pad_start = pad_end - pad_counts
    dest = pad_start[e_s] + (jnp.arange(m, dtype=jnp.int32) - start[e_s])
    n_blk = (m + N_EXPERTS * (MOE_BLOCK - 1)) // MOE_BLOCK + 1
    rows = jnp.zeros((n_blk * MOE_BLOCK, h.shape[1]), h.dtype).at[dest].set(h[tok_s])
    blk_start = jnp.arange(n_blk, dtype=jnp.int32) * MOE_BLOCK
    blk_e = jnp.minimum(jnp.sum(pad_end[None, :] <= blk_start[:, None], axis=1), N_EXPERTS - 1)

    def expert_block(args):
        xb, e = args
        gu = xb @ w1[e] + b1[e]
        g_, u_ = gu[:, :EXPERT_FF], gu[:, EXPERT_FF:]
        g_ = jnp.minimum(g_, SWIGLU_LIMIT)
        u_ = jnp.clip(u_, -SWIGLU_LIMIT, SWIGLU_LIMIT)
        act = (u_ + 1.0) * g_ * jax.nn.sigmoid(SWIGLU_ALPHA * g_)
        return act @ w2[e] + b2[e]

    ys = lax.map(expert_block, (rows.reshape(n_blk, MOE_BLOCK, h.shape[1]), blk_e))
    ys = ys.reshape(n_blk * MOE_BLOCK, h.shape[1])
    y = jax.ops.segment_sum(ys[dest].astype(jnp.float32) * g_s[:, None], tok_s, num_segments=n_tok)
    return y.astype(h.dtype)


def _moe(x, lw):
    h = _rmsnorm(x, lw['norm_ffn']).reshape(-1, x.shape[-1])
    y = _routed_experts(h, lw['router_w'], lw['router_b'], lw['moe_w1'], lw['moe_b1'],
                        lw['moe_w2'], lw['moe_b2'])
    return x + y.reshape(x.shape)


def _layer(x, s_gla, s_ssd, s_conv, mem_k, mem_v, lw):
    x, s_gla, s_ssd, s_conv = _mixer(x, s_gla, s_ssd, s_conv, lw)
    x = _cross(x, mem_k, mem_v, lw)
    x = _moe(x, lw)
    return x, s_gla, s_ssd, s_conv


def setup_inputs(seed: int = 0) -> dict:
    key = jax.random.key(seed)
    ks = iter(jax.random.split(key, 48))
    f32 = jnp.float32
    L = DEPTH

    def nrm(shape, scale):
        return jax.random.normal(next(ks), shape, f32) * scale

    def gain(shape):
        return 1.0 + nrm(shape, 0.02)

    dt0 = jnp.exp(jax.random.uniform(next(ks), (L, SSD_HEADS), f32, math.log(1e-3), math.log(1e-1)))
    return {
        'x_prompt': nrm((BATCH, SEQ, D_MODEL), 1.0),
        'x_sample': nrm((DEC_BATCH, DEC_SEQ, D_MODEL), 1.0),
        'mem_prompt': nrm((BATCH, MEM_LEN, D_MODEL), 1.0),
        'state_gla': nrm((L, DEC_BATCH, GLA_HEADS, GLA_DK, GLA_DV), 0.5),
        'state_ssd': nrm((L, DEC_BATCH, SSD_HEADS, SSD_HEADDIM, SSD_DSTATE), 0.5),
        'state_conv': nrm((L, DEC_BATCH, SSD_CONV_W - 1, SSD_CONV_DIM), 1.0),
        'cache_mem_k': nrm((L, DEC_BATCH, MEM_LEN, MEM_HEADS, MEM_HD), 1.0),
        'cache_mem_v': nrm((L, DEC_BATCH, MEM_LEN, MEM_HEADS, MEM_HD), 1.0),
        'norm_mix': gain((L, D_MODEL)),
        'w_in': nrm((L, D_MODEL, D_IN), D_MODEL ** -0.5),
        'gla_gate_w': nrm((L, GLA_GATE_RANK, GLA_KEY_W), GLA_GATE_RANK ** -0.5),
        'gla_gate_b': 2.0 + nrm((L, GLA_KEY_W), 0.5),
        'gla_norm': gain((L, GLA_DV)),
        'ssd_conv_w': nrm((L, SSD_CONV_W, SSD_CONV_DIM), SSD_CONV_W ** -0.5),
        'ssd_conv_b': nrm((L, SSD_CONV_DIM), 0.02),
        'ssd_dt_bias': dt0 + jnp.log(-jnp.expm1(-dt0)),
        'ssd_a_log': jnp.log(jax.random.uniform(next(ks), (L, SSD_HEADS), f32, 1.0, 16.0)),
        'ssd_d': 1.0 + nrm((L, SSD_HEADS), 0.1),
        'ssd_norm': gain((L, SSD_INNER)),
        'w_proj_gla': nrm((L, GLA_VAL_W, D_MODEL), GLA_VAL_W ** -0.5),
        'w_proj_ssd': nrm((L, SSD_INNER, D_MODEL), SSD_INNER ** -0.5),
        'w_out': nrm((L, D_MODEL, D_MODEL), D_MODEL ** -0.5),
        'norm_cross': gain((L, D_MODEL)),
        'norm_mem': gain((L, D_MODEL)),
        'w_cq': nrm((L, D_MODEL, MEM_W), D_MODEL ** -0.5),
        'w_ck': nrm((L, D_MODEL, MEM_W), D_MODEL ** -0.5),
        'w_cv': nrm((L, D_MODEL, MEM_W), D_MODEL ** -0.5),
        'w_co': nrm((L, MEM_W, D_MODEL), MEM_W ** -0.5),
        'norm_ffn': gain((L, D_MODEL)),
        'router_w': nrm((L, D_MODEL, N_EXPERTS), D_MODEL ** -0.5),
        'router_b': nrm((L, N_EXPERTS), 0.01),
        'moe_w1': nrm((L, N_EXPERTS, D_MODEL, 2 * EXPERT_FF), D_MODEL ** -0.5),
        'moe_b1': nrm((L, N_EXPERTS, 2 * EXPERT_FF), 0.01),
        'moe_w2': nrm((L, N_EXPERTS, EXPERT_FF, D_MODEL), EXPERT_FF ** -0.5),
        'moe_b2': nrm((L, N_EXPERTS, D_MODEL), 0.01),
        'norm_final': gain((D_MODEL,)),
    }


def reference(x_prompt, x_sample, mem_prompt, state_gla, state_ssd, state_conv, cache_mem_k, cache_mem_v,
              norm_mix, w_in, gla_gate_w, gla_gate_b, gla_norm, ssd_conv_w, ssd_conv_b, ssd_dt_bias,
              ssd_a_log, ssd_d, ssd_norm, w_proj_gla, w_proj_ssd, w_out, norm_cross, norm_mem, w_cq,
              w_ck, w_cv, w_co, norm_ffn, router_w, router_b, moe_w1, moe_b1, moe_w2, moe_b2, norm_final):
    bp = x_prompt.shape[0]
    dtp = x_prompt.dtype
    xp, xs = x_prompt, x_sample
    gla_p, ssd_p, conv_p, mk_p, mv_p = [], [], [], [], []
    gla_s, ssd_s, conv_s = [], [], []
    for l in range(DEPTH):
        lw = dict(norm_mix=norm_mix[l], w_in=w_in[l], gla_gate_w=gla_gate_w[l], gla_gate_b=gla_gate_b[l],
                  gla_norm=gla_norm[l], ssd_conv_w=ssd_conv_w[l], ssd_conv_b=ssd_conv_b[l],
                  ssd_dt_bias=ssd_dt_bias[l], ssd_a_log=ssd_a_log[l], ssd_d=ssd_d[l], ssd_norm=ssd_norm[l],
                  w_proj_gla=w_proj_gla[l], w_proj_ssd=w_proj_ssd[l], w_out=w_out[l],
                  norm_cross=norm_cross[l], w_cq=w_cq[l], w_co=w_co[l], norm_ffn=norm_ffn[l],
                  router_w=router_w[l], router_b=router_b[l], moe_w1=moe_w1[l], moe_b1=moe_b1[l],
                  moe_w2=moe_w2[l], moe_b2=moe_b2[l])
        mk, mv = _memory_kv(mem_prompt, norm_mem[l], w_ck[l], w_cv[l])
        xp, g1, s1, c1 = _layer(
            xp, jnp.zeros((bp, GLA_HEADS, GLA_DK, GLA_DV), dtp),
            jnp.zeros((bp, SSD_HEADS, SSD_HEADDIM, SSD_DSTATE), dtp),
            jnp.zeros((bp, SSD_CONV_W - 1, SSD_CONV_DIM), dtp), mk, mv, lw)
        gla_p.append(g1); ssd_p.append(s1); conv_p.append(c1); mk_p.append(mk); mv_p.append(mv)
        xs, g2, s2, c2 = _layer(xs, state_gla[l], state_ssd[l], state_conv[l],
                                cache_mem_k[l], cache_mem_v[l], lw)
        gla_s.append(g2); ssd_s.append(s2); conv_s.append(c2)
    y_prompt = _rmsnorm(xp, norm_final)
    y_sample = _rmsnorm(xs, norm_final)
    return (y_prompt, y_sample, jnp.stack(gla_p), jnp.stack(ssd_p), jnp.stack(conv_p),
            jnp.stack(mk_p), jnp.stack(mv_p), jnp.stack(gla_s), jnp.stack(ssd_s), jnp.stack(conv_s))
```

```python
import functools
import math

import jax
import jax.numpy as jnp
from jax import lax
from jax.experimental import pallas as pl
from jax.experimental.pallas import tpu as pltpu

F32 = jnp.float32
BF16 = jnp.bfloat16

RMS_EPS = 1e-6
GLA_GATE_TAU = 16.0
GLA_CHUNK = 64
SSD_CHUNK = 64
TOP_K = 4
SWIGLU_LIMIT = 7.0
SWIGLU_ALPHA = 1.702

V7X_LANES = 128
V7X_SUBLANES = 8
V7X_VMEM_BYTES = 64 * 1024 * 1024
VMEM_LIMIT = V7X_VMEM_BYTES - 8 * 1024 * 1024

NT_DIMS = (((1,), (1,)), ((), ()))
TN_DIMS = (((0,), (0,)), ((), ()))


def _cparams(*sem):
    return pltpu.CompilerParams(dimension_semantics=sem, vmem_limit_bytes=VMEM_LIMIT)


def _tile(n, pref, quantum):
    t = (min(pref, n) // quantum) * quantum
    while t >= quantum:
        if n % t == 0:
            return t
        t -= quantum
    return n


def _dot(a, b):
    return jnp.dot(a.astype(BF16), b.astype(BF16), preferred_element_type=F32)


def _dot_nt(a, b):
    return lax.dot_general(a.astype(BF16), b.astype(BF16), NT_DIMS, preferred_element_type=F32)


def _dot_tn(a, b):
    return lax.dot_general(a.astype(BF16), b.astype(BF16), TN_DIMS, preferred_element_type=F32)


def _split3(x):
    hi = x.astype(BF16)
    r1 = x - hi.astype(F32)
    mid = r1.astype(BF16)
    lo = (r1 - mid.astype(F32)).astype(BF16)
    return hi, mid, lo


def _cumsum_rows(x, tri_bf):
    hi, mid, lo = _split3(x)
    n = x.shape[1]
    cat = jnp.concatenate([hi, mid, lo], axis=1)
    s = jnp.dot(tri_bf, cat, preferred_element_type=F32)
    return s[:, :n] + s[:, n:2 * n] + s[:, 2 * n:]


def _softplus(x):
    return jnp.maximum(x, 0.0) + jnp.log1p(jnp.exp(-jnp.abs(x)))


def _silu(x):
    return x * jax.nn.sigmoid(x)


def _tri_masks(c):
    row = lax.broadcasted_iota(jnp.int32, (c, c), 0)
    col = lax.broadcasted_iota(jnp.int32, (c, c), 1)
    mask = row >= col
    return mask, jnp.where(mask, 1.0, 0.0).astype(BF16)


def _rms_kernel(x_ref, g_ref, o_ref):
    x = x_ref[...]
    ms = jnp.mean(x * x, axis=-1, keepdims=True)
    o_ref[...] = (x * lax.rsqrt(ms + RMS_EPS) * g_ref[...]).astype(o_ref.dtype)


def _rmsnorm(x, g, out_dtype):
    n, d = x.shape
    tm = _tile(n, 256, V7X_SUBLANES)
    return pl.pallas_call(
        _rms_kernel,
        grid=(n // tm,),
        in_specs=[pl.BlockSpec((tm, d), lambda i: (i, 0)),
                  pl.BlockSpec((1, d), lambda i: (0, 0))],
        out_specs=pl.BlockSpec((tm, d), lambda i: (i, 0)),
        out_shape=jax.ShapeDtypeStruct((n, d), out_dtype),
        compiler_params=_cparams("parallel"),
        name="rmsnorm",
    )(x, g.reshape(1, d))


def _mm_kernel(*refs, has_res):
    if has_res:
        x_ref, w_ref, r_ref, o_ref, wb_ref = refs
    else:
        x_ref, w_ref, o_ref, wb_ref = refs
        r_ref = None

    @pl.when(pl.program_id(1) == 0)
    def _():
        wb_ref[...] = w_ref[...].astype(BF16)

    acc = jnp.dot(x_ref[...], wb_ref[...], preferred_element_type=F32)
    if r_ref is not None:
        acc = acc + r_ref[...]
    o_ref[...] = acc.astype(o_ref.dtype)


def _matmul(x, w, *, col0=0, ncols=None, res=None, out_dtype=F32, tm=512, tn=512):
    n, k = x.shape
    ncols = w.shape[1] - col0 if ncols is None else ncols
    tm = _tile(n, tm, V7X_SUBLANES)
    tn = _tile(math.gcd(ncols, col0) if col0 else ncols, tn, V7X_LANES)
    assert ncols % tn == 0 and col0 % tn == 0
    jb = col0 // tn
    in_specs = [pl.BlockSpec((tm, k), lambda j, i: (i, 0)),
                pl.BlockSpec((k, tn), lambda j, i: (0, j + jb))]
    args = [x, w]
    if res is not None:
        in_specs.append(pl.BlockSpec((tm, tn), lambda j, i: (i, j)))
        args.append(res)
    return pl.pallas_call(
        functools.partial(_mm_kernel, has_res=res is not None),
        grid=(ncols // tn, n // tm),
        in_specs=in_specs,
        out_specs=pl.BlockSpec((tm, tn), lambda j, i: (i, j)),
        out_shape=jax.ShapeDtypeStruct((n, ncols), out_dtype),
        scratch_shapes=[pltpu.VMEM((k, tn), BF16)],
        compiler_params=_cparams("arbitrary", "arbitrary"),
        name="matmul",
    )(*args)


def _merge_kernel(o_ref, y_ref, wg_ref, ws_ref, ga_ref, gb_ref, out_ref, wgb_ref, wsb_ref):
    @pl.when(pl.program_id(1) == 0)
    def _():
        wgb_ref[...] = wg_ref[...].astype(BF16)
        wsb_ref[...] = ws_ref[...].astype(BF16)

    pg = jnp.dot(o_ref[...], wgb_ref[...], preferred_element_type=F32)
    ps = jnp.dot(y_ref[...], wsb_ref[...], preferred_element_type=F32)
    out = jax.nn.sigmoid(ga_ref[...]) * pg + jax.nn.sigmoid(gb_ref[...]) * ps
    out_ref[...] = out.astype(out_ref.dtype)


def _merge(o_gla, y_ssd, w_pg, w_ps, gates, tm=512, tn=256):
    n, k = o_gla.shape
    d = w_pg.shape[1]
    tm = _tile(n, tm, V7X_SUBLANES)
    tn = _tile(d, tn, V7X_LANES)
    nj = d // tn
    return pl.pallas_call(
        _merge_kernel,
        grid=(nj, n // tm),
        in_specs=[pl.BlockSpec((tm, k), lambda j, i: (i, 0)),
                  pl.BlockSpec((tm, k), lambda j, i: (i, 0)),
                  pl.BlockSpec((k, tn), lambda j, i: (0, j)),
                  pl.BlockSpec((k, tn), lambda j, i: (0, j)),
                  pl.BlockSpec((tm, tn), lambda j, i: (i, j)),
                  pl.BlockSpec((tm, tn), lambda j, i: (i, j + nj))],
        out_specs=pl.BlockSpec((tm, tn), lambda j, i: (i, j)),
        out_shape=jax.ShapeDtypeStruct((n, d), BF16),
        scratch_shapes=[pltpu.VMEM((k, tn), BF16), pltpu.VMEM((k, tn), BF16)],
        compiler_params=_cparams("arbitrary", "arbitrary"),
        name="merge",
    )(o_gla, y_ssd, w_pg, w_ps, gates, gates)


def _gla_kernel(*refs, heads, dk, dv, c, t_valid, lr_off, rank, has_s0):
    if has_s0:
        q_ref, k_ref, v_ref, r_ref, lr_ref, gw_ref, gb_ref, gn_ref, s0_ref, o_ref, so_ref, s_scr = refs
    else:
        q_ref, k_ref, v_ref, r_ref, lr_ref, gw_ref, gb_ref, gn_ref, o_ref, so_ref, s_scr = refs
        s0_ref = None
    ci = pl.program_id(1)

    @pl.when(ci == 0)
    def _():
        if s0_ref is None:
            s_scr[...] = jnp.zeros_like(s_scr)
        else:
            s_scr[...] = s0_ref[...]

    mask, tri_bf = _tri_masks(c)
    lr = lr_ref[:, lr_off:lr_off + rank]
    scale = dk ** -0.5
    if t_valid < c:
        live = lax.broadcasted_iota(jnp.int32, (c, dk), 0) < t_valid
    gn = gn_ref[...]
    for h in range(heads):
        ks = slice(h * dk, (h + 1) * dk)
        vs = slice(h * dv, (h + 1) * dv)
        q = q_ref[:, ks] * scale
        k = k_ref[:, ks]
        v = v_ref[:, vs]
        zg = jnp.dot(lr, gw_ref[:, ks], preferred_element_type=F32) + gb_ref[:, ks]
        la = (jnp.minimum(zg, 0.0) - jnp.log1p(jnp.exp(-jnp.abs(zg)))) * (1.0 / GLA_GATE_TAU)
        if t_valid < c:
            la = jnp.where(live, la, 0.0)
        b = _cumsum_rows(la, tri_bf)
        qe = q * jnp.exp(b)
        ke = k * jnp.exp(-b)
        att = jnp.where(mask, _dot_nt(qe, ke), 0.0)
        s_old = s_scr[h]
        o = _dot(att, v) + _dot(qe, s_old)
        b_last = b[c - 1:c, :]
        kd = k * jnp.exp(b_last - b)
        e_col = jnp.broadcast_to(jnp.exp(b_last), (dk, dk)).T
        e_full = jnp.concatenate([e_col] * (dv // dk), axis=1)
        s_scr[h] = e_full * s_old + _dot_tn(kd, v)
        ms = jnp.mean(o * o, axis=-1, keepdims=True)
        og = o * lax.rsqrt(ms + RMS_EPS) * gn * _silu(r_ref[:, vs])
        o_ref[:, vs] = og.astype(o_ref.dtype)

    @pl.when(ci == pl.num_programs(1) - 1)
    def _():
        so_ref[...] = s_scr[...]


def _gla_call(srcs, gate_w, gate_b, gnorm, s0, *, heads, batch, n_chunks, c, t_valid, lr_off, out_rows):
    rank, kw = gate_w.shape
    dk = kw // heads
    dv = gnorm.shape[-1]
    vw = heads * dv

    def spec(name):
        arr, width, cb = srcs[name]
        if arr.ndim == 3:
            return pl.BlockSpec((None, c, width), lambda b, ci: (b, ci, cb))
        return pl.BlockSpec((c, width), lambda b, ci: (b * n_chunks + ci, cb))

    names = ("q", "k", "v", "r", "lr")
    in_specs = [spec(nm) for nm in names] + [
        pl.BlockSpec((rank, kw), lambda b, ci: (0, 0)),
        pl.BlockSpec((1, kw), lambda b, ci: (0, 0)),
        pl.BlockSpec((1, dv), lambda b, ci: (0, 0)),
    ]
    args = [srcs[nm][0] for nm in names] + [gate_w, gate_b.reshape(1, kw), gnorm.reshape(1, dv)]
    if s0 is not None:
        in_specs.append(pl.BlockSpec((None, heads, dk, dv), lambda b, ci: (b, 0, 0, 0)))
        args.append(s0)
    three_d = srcs["q"][0].ndim == 3
    if three_d:
        o_spec = pl.BlockSpec((None, c, vw), lambda b, ci: (b, ci, 0))
        o_shape = jax.ShapeDtypeStruct((batch, n_chunks * c, vw), BF16)
    else:
        o_spec = pl.BlockSpec((c, vw), lambda b, ci: (b * n_chunks + ci, 0))
        o_shape = jax.ShapeDtypeStruct((out_rows, vw), BF16)
    return pl.pallas_call(
        functools.partial(_gla_kernel, heads=heads, dk=dk, dv=dv, c=c, t_valid=t_valid,
                          lr_off=lr_off, rank=rank, has_s0=s0 is not None),
        grid=(batch, n_chunks),
        in_specs=in_specs,
        out_specs=[o_spec, pl.BlockSpec((None, heads, dk, dv), lambda b, ci: (b, 0, 0, 0))],
        out_shape=[o_shape, jax.ShapeDtypeStruct((batch, heads, dk, dv), F32)],
        scratch_shapes=[pltpu.VMEM((heads, dk, dv), F32)],
        compiler_params=_cparams("parallel", "arbitrary"),
        name="gla",
    )(*args)


def _ssd_kernel(*refs, heads, groups, p, n, c, t_valid, dt_off, conv_w, has_state):
    if has_state:
        (x_ref, bc_ref, z_ref, dt_ref, cw_ref, cb_ref, dtb_ref, alog_ref, dexp_ref, nrm_ref,
         cs_ref, h0_ref, y_ref, ho_ref, h_scr, xf_scr, bcf_scr, y_scr) = refs
    else:
        (x_ref, bc_ref, z_ref, dt_ref, cw_ref, cb_ref, dtb_ref, alog_ref, dexp_ref, nrm_ref,
         y_ref, ho_ref, h_scr, xf_scr, bcf_scr, y_scr) = refs
        cs_ref = h0_ref = None
    ci = pl.program_id(1)
    inner = heads * p
    gn = groups * n
    r_per_g = heads // groups
    pad = V7X_SUBLANES
    hist = conv_w - 1

    @pl.when(ci == 0)
    def _():
        if has_state:
            h_scr[...] = h0_ref[...]
            xf_scr[0:pad, :] = jnp.zeros((pad, inner), F32)
            bcf_scr[0:pad, :] = jnp.zeros((pad, 2 * gn), F32)
            xf_scr[pad - hist:pad, :] = cs_ref[:, 0:inner]
            bcf_scr[pad - hist:pad, :] = cs_ref[:, inner:inner + 2 * gn]
        else:
            h_scr[...] = jnp.zeros_like(h_scr)
            xf_scr[0:pad, :] = jnp.zeros((pad, inner), F32)
            bcf_scr[0:pad, :] = jnp.zeros((pad, 2 * gn), F32)

    xf_scr[pad:pad + c, :] = x_ref[...]
    bcf_scr[pad:pad + c, :] = bc_ref[...]
    xc = cb_ref[:, 0:inner] + xf_scr[pad - hist:pad - hist + c, :] * cw_ref[0:1, 0:inner]
    bcc = cb_ref[:, inner:] + bcf_scr[pad - hist:pad - hist + c, :] * cw_ref[0:1, inner:]
    for j in range(1, conv_w):
        xc = xc + xf_scr[pad - hist + j:pad - hist + j + c, :] * cw_ref[j:j + 1, 0:inner]
        bcc = bcc + bcf_scr[pad - hist + j:pad - hist + j + c, :] * cw_ref[j:j + 1, inner:]
    xs = _silu(xc)
    bcs = _silu(bcc)
    xf_scr[0:pad, :] = x_ref[c - pad:c, :]
    bcf_scr[0:pad, :] = bc_ref[c - pad:c, :]

    dt = _softplus(dt_ref[:, dt_off:dt_off + heads] + dtb_ref[...])
    if t_valid < c:
        dt = jnp.where(lax.broadcasted_iota(jnp.int32, (c, heads), 0) < t_valid, dt, 0.0)
    a_neg = -jnp.exp(alog_ref[...])
    mask, tri_bf = _tri_masks(c)
    cum = _cumsum_rows(dt * a_neg, tri_bf)
    sq = V7X_LANES
    both = jnp.concatenate([cum, dt], axis=1)
    if both.shape[1] < sq:
        both = jnp.concatenate([both, jnp.zeros((c, sq - both.shape[1]), F32)], axis=1)
    if c < sq:
        both = jnp.concatenate([both, jnp.zeros((sq - c, sq), F32)], axis=0)
    both_t = both.T
    e_cum = jnp.exp(cum)
    last = cum[c - 1:c, :]
    wk = jnp.exp(last - cum) * dt
    e_last = jnp.exp(last)

    for g in range(groups):
        bg = bcs[:, g * n:(g + 1) * n]
        cg = bcs[:, gn + g * n:gn + (g + 1) * n]
        cbm = _dot_nt(cg, bg)
        hg = h_scr[g]
        ych = _dot_nt(cg, hg)
        xw_parts = []
        for r in range(r_per_g):
            hd = g * r_per_g + r
            cum_col = cum[:, hd:hd + 1]
            cum_row = both_t[hd:hd + 1, 0:c]
            dt_row = both_t[heads + hd:heads + hd + 1, 0:c]
            decay = jnp.exp(jnp.where(mask, cum_col - cum_row, -jnp.inf))
            w = cbm * decay * dt_row
            xh = xs[:, hd * p:(hd + 1) * p]
            yh = _dot(w, xh) + ych[:, r * p:(r + 1) * p] * e_cum[:, hd:hd + 1]
            y_scr[:, hd * p:(hd + 1) * p] = yh
            xw_parts.append(xh * wk[:, hd:hd + 1])
        xw = jnp.concatenate(xw_parts, axis=1)
        upd = _dot_tn(xw, bg)
        for r in range(r_per_g):
            hd = g * r_per_g + r
            rs = slice(r * p, (r + 1) * p)
            h_scr[g, rs, :] = e_last[:, hd:hd + 1] * hg[rs, :] + upd[rs, :]

    y = (y_scr[...] + dexp_ref[...] * xs) * _silu(z_ref[...])
    gw = inner // groups
    for g in range(groups):
        gs = slice(g * gw, (g + 1) * gw)
        yg = y[:, gs]
        ms = jnp.mean(yg * yg, axis=-1, keepdims=True)
        y_ref[:, gs] = (yg * lax.rsqrt(ms + RMS_EPS) * nrm_ref[:, gs]).astype(y_ref.dtype)

    @pl.when(ci == pl.num_programs(1) - 1)
    def _():
        ho_ref[...] = h_scr[...]


def _ssd_call(srcs, conv_w, conv_b, dt_bias, a_log, d_skip, ssd_norm, conv_state, h0, *,
              heads, groups, n, batch, n_chunks, c, t_valid, dt_off, out_rows):
    kw, conv_dim = conv_w.shape
    inner = ssd_norm.shape[-1]
    p = inner // heads
    gn = groups * n
    r_per_g = heads // groups

    def spec(name):
        arr, width, cb = srcs[name]
        if arr.ndim == 3:
            return pl.BlockSpec((None, c, width), lambda b, ci: (b, ci, cb))
        return pl.BlockSpec((c, width), lambda b, ci: (b * n_chunks + ci, cb))

    def full(shape):
        return pl.BlockSpec(shape, lambda b, ci: (0,) * len(shape))

    names = ("x", "bc", "z", "dt")
    in_specs = [spec(nm) for nm in names] + [
        full((kw, conv_dim)), full((1, conv_dim)), full((1, heads)), full((1, heads)),
        full((1, inner)), full((1, inner))]
    args = [srcs[nm][0] for nm in names] + [
        conv_w, conv_b.reshape(1, conv_dim), dt_bias.reshape(1, heads), a_log.reshape(1, heads),
        jnp.repeat(d_skip, p).reshape(1, inner), ssd_norm.reshape(1, inner)]
    has_state = h0 is not None
    if has_state:
        in_specs += [pl.BlockSpec((None, kw - 1, conv_dim), lambda b, ci: (b, 0, 0)),
                     pl.BlockSpec((None, groups, r_per_g * p, n), lambda b, ci: (b, 0, 0, 0))]
        args += [conv_state, h0.reshape(batch, groups, r_per_g * p, n)]
    three_d = srcs["x"][0].ndim == 3
    if three_d:
        y_spec = pl.BlockSpec((None, c, inner), lambda b, ci: (b, ci, 0))
        y_shape = jax.ShapeDtypeStruct((batch, n_chunks * c, inner), BF16)
    else:
        y_spec = pl.BlockSpec((c, inner), lambda b, ci: (b * n_chunks + ci, 0))
        y_shape = jax.ShapeDtypeStruct((out_rows, inner), BF16)
    y, h_new = pl.pallas_call(
        functools.partial(_ssd_kernel, heads=heads, groups=groups, p=p, n=n, c=c, t_valid=t_valid,
                          dt_off=dt_off, conv_w=kw, has_state=has_state),
        grid=(batch, n_chunks),
        in_specs=in_specs,
        out_specs=[y_spec, pl.BlockSpec((None, groups, r_per_g * p, n), lambda b, ci: (b, 0, 0, 0))],
        out_shape=[y_shape, jax.ShapeDtypeStruct((batch, groups, r_per_g * p, n), F32)],
        scratch_shapes=[pltpu.VMEM((groups, r_per_g * p, n), F32),
                        pltpu.VMEM((V7X_SUBLANES + c, inner), F32),
                        pltpu.VMEM((V7X_SUBLANES + c, 2 * gn), F32),
                        pltpu.VMEM((c, inner), F32)],
        compiler_params=_cparams("parallel", "arbitrary"),
        name="ssd",
    )(*args)
    return y, h_new.reshape(batch, heads, p, n)


def _attn_kernel(q_ref, k_ref, v_ref, o_ref, *, heads, hd):
    scale = hd ** -0.5
    for h in range(heads):
        hs = slice(h * hd, (h + 1) * hd)
        s = _dot_nt(q_ref[:, hs], k_ref[:, hs]) * scale
        m = jnp.max(s, axis=-1, keepdims=True)
        e = jnp.exp(s - m)
        prob = e / jnp.sum(e, axis=-1, keepdims=True)
        o_ref[:, hs] = _dot(prob, v_ref[:, hs]).astype(o_ref.dtype)


def _attention(q, mem_k, mem_v, *, heads, tq=512):
    batch, t, w = q.shape
    m = mem_k.shape[1]
    tq = _tile(t, tq, V7X_SUBLANES)
    return pl.pallas_call(
        functools.partial(_attn_kernel, heads=heads, hd=w // heads),
        grid=(batch, t // tq),
        in_specs=[pl.BlockSpec((None, tq, w), lambda b, i: (b, i, 0)),
                  pl.BlockSpec((None, m, w), lambda b, i: (b, 0, 0)),
                  pl.BlockSpec((None, m, w), lambda b, i: (b, 0, 0))],
        out_specs=pl.BlockSpec((None, tq, w), lambda b, i: (b, i, 0)),
        out_shape=jax.ShapeDtypeStruct((batch, t, w), BF16),
        compiler_params=_cparams("parallel", "arbitrary"),
        name="cross_attention",
    )(q, mem_k, mem_v)


def _router_kernel(x_ref, g_ref, rw_ref, rb_ref, h_ref, idx_ref, gate_ref, *, n_exp, top_k):
    x = x_ref[...]
    ms = jnp.mean(x * x, axis=-1, keepdims=True)
    h = x * lax.rsqrt(ms + RMS_EPS) * g_ref[...]
    h_ref[...] = h
    h_hi, h_mid, h_lo = _split3(h)
    w_hi, w_mid, w_lo = _split3(rw_ref[...])

    def d(a, b):
        return jnp.dot(a, b, preferred_element_type=F32)

    logits = (d(h_hi, w_hi) + (d(h_hi, w_mid) + d(h_mid, w_hi))
              + (d(h_hi, w_lo) + d(h_mid, w_mid) + d(h_lo, w_hi))) + rb_ref[...]
    lanes = lax.broadcasted_iota(jnp.int32, logits.shape, 1).astype(F32)
    work = logits
    vals, idxs = [], []
    for _ in range(top_k):
        mx = jnp.max(work, axis=-1, keepdims=True)
        ix = jnp.min(jnp.where(work == mx, lanes, float(n_exp)), axis=-1, keepdims=True)
        vals.append(mx)
        idxs.append(ix)
        work = jnp.where(lanes == ix, -jnp.inf, work)
    es = [jnp.exp(v - vals[0]) for v in vals]
    tot = es[0]
    for e in es[1:]:
        tot = tot + e
    out_lanes = lax.broadcasted_iota(jnp.int32, idx_ref.shape, 1)
    idx_out = jnp.zeros(idx_ref.shape, jnp.int32)
    gate_out = jnp.zeros(gate_ref.shape, F32)
    for j in range(top_k):
        idx_out = jnp.where(out_lanes == j, idxs[j].astype(jnp.int32), idx_out)
        gate_out = jnp.where(out_lanes == j, es[j] / tot, gate_out)
    idx_ref[...] = idx_out
    gate_ref[...] = gate_out


def _router(x, g, router_w, router_b):
    n, d = x.shape
    n_exp = router_w.shape[1]
    tm = _tile(n, 256, V7X_SUBLANES)
    return pl.pallas_call(
        functools.partial(_router_kernel, n_exp=n_exp, top_k=TOP_K),
        grid=(n // tm,),
        in_specs=[pl.BlockSpec((tm, d), lambda i: (i, 0)),
                  pl.BlockSpec((1, d), lambda i: (0, 0)),
                  pl.BlockSpec((d, n_exp), lambda i: (0, 0)),
                  pl.BlockSpec((1, n_exp), lambda i: (0, 0))],
        out_specs=[pl.BlockSpec((tm, d), lambda i: (i, 0)),
                   pl.BlockSpec((tm, V7X_LANES), lambda i: (i, 0)),
                   pl.BlockSpec((tm, V7X_LANES), lambda i: (i, 0))],
        out_shape=[jax.ShapeDtypeStruct((n, d), F32),
                   jax.ShapeDtypeStruct((n, V7X_LANES), jnp.int32),
                   jax.ShapeDtypeStruct((n, V7X_LANES), F32)],
        compiler_params=_cparams("parallel"),
        name="router",
    )(x, g.reshape(1, d), router_w, router_b.reshape(1, n_exp))


def _row_copy(src_hbm, row, dst, slot, sem):
    return pltpu.make_async_copy(src_hbm.at[pl.ds(row, 1), :], dst.at[pl.ds(slot, 1), :], sem)


def _gather_kernel(idx_ref, h_hbm, o_ref, buf, sem, *, tg):
    def start(r, carry):
        _row_copy(h_hbm, idx_ref[0, 0, r], buf, r, sem).start()
        return carry

    lax.fori_loop(0, tg, start, 0)

    def wait(r, carry):
        _row_copy(h_hbm, 0, buf, r, sem).wait()
        return carry

    lax.fori_loop(0, tg, wait, 0)
    o_ref[...] = buf[...].astype(o_ref.dtype)


def _gather_rows(h, src, tg=256):
    m = src.shape[0]
    d = h.shape[1]
    return pl.pallas_call(
        functools.partial(_gather_kernel, tg=tg),
        grid=(m // tg,),
        in_specs=[pl.BlockSpec((1, 1, tg), lambda i: (i, 0, 0), memory_space=pltpu.SMEM),
                  pl.BlockSpec(memory_space=pl.ANY)],
        out_specs=pl.BlockSpec((tg, d), lambda i: (i, 0)),
        out_shape=jax.ShapeDtypeStruct((m, d), BF16),
        scratch_shapes=[pltpu.VMEM((tg, d), F32), pltpu.SemaphoreType.DMA(())],
        compiler_params=_cparams("arbitrary"),
        name="moe_gather",
    )(src.reshape(m // tg, 1, tg), h)


def _new_weights(be_ref):
    i = pl.program_id(1)
    prev = be_ref[jnp.maximum(i - 1, 0)]
    return jnp.logical_or(i == 0, be_ref[i] != prev)


def _ffn1_kernel(be_ref, nu_ref, x_ref, wg_ref, wu_ref, bg_ref, bu_ref, o_ref, wgb_ref, wub_ref):
    @pl.when(_new_weights(be_ref))
    def _():
        wgb_ref[...] = wg_ref[...].astype(BF16)
        wub_ref[...] = wu_ref[...].astype(BF16)

    @pl.when(pl.program_id(1) < nu_ref[0])
    def _():
        x = x_ref[...]
        g_ = jnp.dot(x, wgb_ref[...], preferred_element_type=F32) + bg_ref[...]
        u_ = jnp.dot(x, wub_ref[...], preferred_element_type=F32) + bu_ref[...]
        g_ = jnp.minimum(g_, SWIGLU_LIMIT)
        u_ = jnp.clip(u_, -SWIGLU_LIMIT, SWIGLU_LIMIT)
        act = (u_ + 1.0) * g_ * jax.nn.sigmoid(SWIGLU_ALPHA * g_)
        o_ref[...] = act.astype(o_ref.dtype)


def _ffn2_kernel(be_ref, nu_ref, x_ref, w_ref, b_ref, o_ref, wb_ref):
    @pl.when(_new_weights(be_ref))
    def _():
        wb_ref[...] = w_ref[...].astype(BF16)

    @pl.when(pl.program_id(1) < nu_ref[0])
    def _():
        o_ref[...] = jnp.dot(x_ref[...], wb_ref[...], preferred_element_type=F32) + b_ref[...]


def _expert_ffn(rows, blk_e, n_used, w1, b1, w2, b2, *, tm, tf=256, tn=512):
    m_pad, d = rows.shape
    n_exp, _, ff2 = w1.shape
    ff = ff2 // 2
    n_blk = m_pad // tm
    tf = _tile(ff, tf, V7X_LANES)
    tn = _tile(d, tn, V7X_LANES)
    nf = ff // tf

    def row_blk(i, nu):
        return jnp.minimum(i, nu[0] - 1)

    act = pl.pallas_call(
        _ffn1_kernel,
        grid_spec=pltpu.PrefetchScalarGridSpec(
            num_scalar_prefetch=2,
            grid=(nf, n_blk),
            in_specs=[pl.BlockSpec((tm, d), lambda j, i, be, nu: (row_blk(i, nu), 0)),
                      pl.BlockSpec((None, d, tf), lambda j, i, be, nu: (be[i], 0, j)),
                      pl.BlockSpec((None, d, tf), lambda j, i, be, nu: (be[i], 0, j + nf)),
                      pl.BlockSpec((None, 1, tf), lambda j, i, be, nu: (be[i], 0, j)),
                      pl.BlockSpec((None, 1, tf), lambda j, i, be, nu: (be[i], 0, j + nf))],
            out_specs=pl.BlockSpec((tm, tf), lambda j, i, be, nu: (row_blk(i, nu), j)),
            scratch_shapes=[pltpu.VMEM((d, tf), BF16), pltpu.VMEM((d, tf), BF16)]),
        out_shape=jax.ShapeDtypeStruct((m_pad, ff), BF16),
        compiler_params=_cparams("arbitrary", "arbitrary"),
        name="moe_ffn1",
    )(blk_e, n_used, rows, w1, w1, b1.reshape(n_exp, 1, ff2), b1.reshape(n_exp, 1, ff2))

    return pl.pallas_call(
        _ffn2_kernel,
        grid_spec=pltpu.PrefetchScalarGridSpec(
            num_scalar_prefetch=2,
            grid=(d // tn, n_blk),
            in_specs=[pl.BlockSpec((tm, ff), lambda j, i, be, nu: (row_blk(i, nu), 0)),
                      pl.BlockSpec((None, ff, tn), lambda j, i, be, nu: (be[i], 0, j)),
                      pl.BlockSpec((None, 1, tn), lambda j, i, be, nu: (be[i], 0, j))],
            out_specs=pl.BlockSpec((tm, tn), lambda j, i, be, nu: (row_blk(i, nu), j)),
            scratch_shapes=[pltpu.VMEM((ff, tn), BF16)]),
        out_shape=jax.ShapeDtypeStruct((m_pad, d), F32),
        compiler_params=_cparams("arbitrary", "arbitrary"),
        name="moe_ffn2",
    )(blk_e, n_used, act, w2, b2.reshape(n_exp, 1, d))


def _combine_kernel(dest_ref, ys_hbm, gate_ref, x_ref, g_ref, o_ref, buf, sem, *, tc, top_k):
    def start(r, carry):
        for j in range(top_k):
            _row_copy(ys_hbm, dest_ref[0, 0, r * top_k + j], buf.at[j], r, sem).start()
        return carry

    lax.fori_loop(0, tc, start, 0)

    def wait(r, carry):
        for j in range(top_k):
            _row_copy(ys_hbm, 0, buf.at[j], r, sem).wait()
        return carry

    lax.fori_loop(0, tc, wait, 0)
    gate = gate_ref[...]
    y = buf[0] * gate[:, 0:1]
    for j in range(1, top_k):
        y = y + buf[j] * gate[:, j:j + 1]
    x = x_ref[...] + y
    ms = jnp.mean(x * x, axis=-1, keepdims=True)
    o_ref[...] = x * lax.rsqrt(ms + RMS_EPS) * g_ref[...]


def _combine(ys, dest, gate, x, g, tc=128):
    n, d = x.shape
    tc = _tile(n, tc, V7X_SUBLANES)
    return pl.pallas_call(
        functools.partial(_combine_kernel, tc=tc, top_k=TOP_K),
        grid=(n // tc,),
        in_specs=[pl.BlockSpec((1, 1, tc * TOP_K), lambda i: (i, 0, 0), memory_space=pltpu.SMEM),
                  pl.BlockSpec(memory_space=pl.ANY),
                  pl.BlockSpec((tc, V7X_LANES), lambda i: (i, 0)),
                  pl.BlockSpec((tc, d), lambda i: (i, 0)),
                  pl.BlockSpec((1, d), lambda i: (0, 0))],
        out_specs=pl.BlockSpec((tc, d), lambda i: (i, 0)),
        out_shape=jax.ShapeDtypeStruct((n, d), F32),
        scratch_shapes=[pltpu.VMEM((TOP_K, tc, d), F32), pltpu.SemaphoreType.DMA(())],
        compiler_params=_cparams("arbitrary"),
        name="moe_combine",
    )(dest.reshape(n // tc, 1, tc * TOP_K), ys, gate, x, g.reshape(1, d))


def _moe_layout(top_i, n_exp, tm):
    n_tok = top_i.shape[0]
    m = n_tok * TOP_K
    e_flat = top_i.reshape(-1)
    onehot = (e_flat[:, None] == jnp.arange(n_exp, dtype=jnp.int32)[None, :]).astype(jnp.int32)
    rank = jnp.sum((jnp.cumsum(onehot, axis=0) - onehot) * onehot, axis=1)
    counts = jnp.sum(onehot, axis=0)
    pad_counts = (counts + tm - 1) // tm * tm
    pad_end = jnp.cumsum(pad_counts)
    pad_start = pad_end - pad_counts
    dest = pad_start[e_flat] + rank
    n_blk = (m + n_exp * (tm - 1)) // tm + 1
    tok = jnp.arange(m, dtype=jnp.int32) // TOP_K
    src = jnp.zeros((n_blk * tm,), jnp.int32).at[dest].set(tok)
    blk_start = jnp.arange(n_blk, dtype=jnp.int32) * tm
    blk_e = jnp.minimum(jnp.sum(pad_end[None, :] <= blk_start[:, None], axis=1), n_exp - 1)
    n_used = (pad_end[-1] // tm).reshape(1)
    return dest.astype(jnp.int32), src, blk_e.astype(jnp.int32), n_used.astype(jnp.int32)


def _pad_seq(x, t_pad):
    b, t, w = x.shape
    return x if t == t_pad else jnp.pad(x, ((0, 0), (0, t_pad - t), (0, 0)))


def kernel(x_prompt, x_sample, mem_prompt, state_gla, state_ssd, state_conv, cache_mem_k, cache_mem_v,
           norm_mix, w_in, gla_gate_w, gla_gate_b, gla_norm, ssd_conv_w, ssd_conv_b, ssd_dt_bias,
           ssd_a_log, ssd_d, ssd_norm, w_proj_gla, w_proj_ssd, w_out, norm_cross, norm_mem, w_cq,
           w_ck, w_cv, w_co, norm_ffn, router_w, router_b, moe_w1, moe_b1, moe_w2, moe_b2, norm_final):
    assert w_in.shape[0] == 1, "single layer"
    bp, tp, d = x_prompt.shape
    bs, ts, _ = x_sample.shape
    n_p, n_s = bp * tp, bs * ts
    n_all = n_p + n_s

    dv = gla_norm.shape[-1]
    val_w = w_proj_gla.shape[1]
    g_heads = val_w // dv
    key_w = gla_gate_w.shape[-1]
    rank = gla_gate_w.shape[1]
    inner = ssd_norm.shape[-1]
    s_heads = ssd_a_log.shape[-1]
    conv_dim = ssd_conv_w.shape[-1]
    n_state = state_ssd.shape[-1]
    s_groups = (conv_dim - inner) // (2 * n_state)
    kw = ssd_conv_w.shape[1]
    mem_heads, mem_hd = cache_mem_k.shape[-2:]
    mem_w = mem_heads * mem_hd
    n_exp = router_w.shape[-1]
    widths = (key_w, key_w, val_w, val_w, rank, inner, conv_dim, s_heads, d, d)
    offs = [0]
    for w_ in widths:
        offs.append(offs[-1] + w_)
    o_q, o_k, o_v, o_r, o_lr, o_z, o_xbc, o_dt, o_ga, o_gb, o_end = offs
    assert key_w == val_w // 2 and inner == d and val_w == d

    w_in2 = w_in.reshape(w_in.shape[1:])
    x_all = jnp.concatenate([x_prompt.reshape(n_p, d), x_sample.reshape(n_s, d)], axis=0)

    h = _rmsnorm(x_all, norm_mix[0], BF16)
    proj_a = _matmul(h, w_in2, col0=0, ncols=o_lr)
    proj_b = _matmul(h, w_in2[:, o_z:o_dt])
    proj_c = _matmul(h, w_in2[:, o_ga:o_end])
    small_w = jnp.concatenate(
        [w_in2[:, o_dt:o_ga], w_in2[:, o_lr:o_z],
         jnp.zeros((d, V7X_LANES - s_heads - rank), F32)], axis=1)
    proj_s = _matmul(h, small_w)
    dt_off, lr_off = 0, s_heads

    gate_w, gate_b = gla_gate_w[0], gla_gate_b[0]
    t_pad = -(-ts // V7X_SUBLANES) * V7X_SUBLANES

    def sample_view(arr):
        return _pad_seq(arr[n_p:].reshape(bs, ts, arr.shape[1]), t_pad)

    kb = key_w
    gla_src_p = {"q": (proj_a, kb, 0), "k": (proj_a, kb, 1), "v": (proj_a, val_w, o_v // val_w),
                 "r": (proj_a, val_w, o_r // val_w), "lr": (proj_s, V7X_LANES, 0)}
    c_p = math.gcd(tp, GLA_CHUNK)
    o_p, gla_p = _gla_call(gla_src_p, gate_w, gate_b, gla_norm[0], None, heads=g_heads, batch=bp,
                           n_chunks=tp // c_p, c=c_p, t_valid=c_p, lr_off=lr_off, out_rows=n_all)
    pa_s, ps_s = sample_view(proj_a), sample_view(proj_s)
    gla_src_s = {"q": (pa_s, kb, 0), "k": (pa_s, kb, 1), "v": (pa_s, val_w, o_v // val_w),
                 "r": (pa_s, val_w, o_r // val_w), "lr": (ps_s, V7X_LANES, 0)}
    o_s, gla_s = _gla_call(gla_src_s, gate_w, gate_b, gla_norm[0], state_gla[0], heads=g_heads, batch=bs,
                           n_chunks=1, c=t_pad, t_valid=ts, lr_off=lr_off, out_rows=None)
    o_all = lax.dynamic_update_slice(o_p, o_s[:, :ts].reshape(n_s, val_w), (n_p, 0))

    bc_w = conv_dim - inner
    ssd_src_p = {"x": (proj_b, inner, 1), "bc": (proj_b, bc_w, (2 * inner) // bc_w),
                 "z": (proj_b, inner, 0), "dt": (proj_s, V7X_LANES, 0)}
    cs_p = math.gcd(tp, SSD_CHUNK)
    y_p, ssd_p = _ssd_call(ssd_src_p, ssd_conv_w[0], ssd_conv_b[0], ssd_dt_bias[0], ssd_a_log[0], ssd_d[0],
                           ssd_norm[0], None, None, heads=s_heads, groups=s_groups, n=n_state, batch=bp,
                           n_chunks=tp // cs_p, c=cs_p, t_valid=cs_p, dt_off=dt_off, out_rows=n_all)
    pb_s = sample_view(proj_b)
    ssd_src_s = {"x": (pb_s, inner, 1), "bc": (pb_s, bc_w, (2 * inner) // bc_w),
                 "z": (pb_s, inner, 0), "dt": (ps_s, V7X_LANES, 0)}
    y_s, ssd_s = _ssd_call(ssd_src_s, ssd_conv_w[0], ssd_conv_b[0], ssd_dt_bias[0], ssd_a_log[0], ssd_d[0],
                           ssd_norm[0], state_conv[0], state_ssd[0], heads=s_heads, groups=s_groups,
                           n=n_state, batch=bs, n_chunks=1, c=t_pad, t_valid=ts, dt_off=dt_off, out_rows=None)
    y_all = lax.dynamic_update_slice(y_p, y_s[:, :ts].reshape(n_s, inner), (n_p, 0))
    assert tp >= kw - 1 and ts >= kw - 1
    xbc_p = proj_b[:n_p].reshape(bp, tp, -1)[:, tp - (kw - 1):, inner:]
    xbc_s = proj_b[n_p:].reshape(bs, ts, -1)[:, ts - (kw - 1):, inner:]

    merged = _merge(o_all, y_all, w_proj_gla[0], w_proj_ssd[0], proj_c)
    x1 = _matmul(merged, w_out[0], res=x_all)

    h2 = _rmsnorm(x1, norm_cross[0], BF16)
    q_all = _matmul(h2, w_cq[0])
    mem = _rmsnorm(mem_prompt.reshape(-1, d), norm_mem[0], BF16)
    mk = _matmul(mem, w_ck[0])
    mv = _matmul(mem, w_cv[0])
    m_len = mem_prompt.shape[1]
    a_p = _attention(q_all[:n_p].reshape(bp, tp, mem_w), mk.reshape(bp, m_len, mem_w),
                     mv.reshape(bp, m_len, mem_w), heads=mem_heads)
    q_s = _pad_seq(q_all[n_p:].reshape(bs, ts, mem_w), t_pad)
    a_s = _attention(q_s, cache_mem_k[0].reshape(bs, -1, mem_w), cache_mem_v[0].reshape(bs, -1, mem_w),
                     heads=mem_heads)
    a_all = jnp.concatenate([a_p.reshape(n_p, mem_w), a_s[:, :ts].reshape(n_s, mem_w)], axis=0)
    x2 = _matmul(a_all, w_co[0], res=x1)

    tm_e = 512
    h3, top_pad, gate_pad = _router(x2, norm_ffn[0], router_w[0], router_b[0])
    dest, src, blk_e, n_used = _moe_layout(top_pad[:, :TOP_K], n_exp, tm_e)
    rows = _gather_rows(h3, src)
    ys = _expert_ffn(rows, blk_e, n_used, moe_w1[0], moe_b1[0], moe_w2[0], moe_b2[0], tm=tm_e)
    y_fin = _combine(ys, dest, gate_pad, x2, norm_final)

    y_prompt = y_fin[:n_p].reshape(bp, tp, d)
    y_sample = y_fin[n_p:].reshape(bs, ts, d)
    return (y_prompt, y_sample, gla_p[None], ssd_p[None], xbc_p[None],
            mk.reshape(1, bp, m_len, mem_heads, mem_hd), mv.reshape(1, bp, m_len, mem_heads, mem_hd),
            gla_s[None], ssd_s[None], xbc_s[None])
```

```python
import functools
import math

import jax
import jax.numpy as jnp
from jax import lax
from jax.experimental import pallas as pl
from jax.experimental.pallas import tpu as pltpu

F32 = jnp.float32
BF16 = jnp.bfloat16

RMS_EPS = 1e-6
GLA_GATE_TAU = 16.0
GLA_CHUNK = 64
SSD_CHUNK = 64
TOP_K = 4
SWIGLU_LIMIT = 7.0
SWIGLU_ALPHA = 1.702

V7X_LANES = 128
V7X_SUBLANES = 8
V7X_VMEM_BYTES = 64 * 1024 * 1024
VMEM_LIMIT = V7X_VMEM_BYTES - 8 * 1024 * 1024

NT_DIMS = (((1,), (1,)), ((), ()))
TN_DIMS = (((0,), (0,)), ((), ()))


def _cparams(*sem):
    return pltpu.CompilerParams(dimension_semantics=sem, vmem_limit_bytes=VMEM_LIMIT)


def _tile(n, pref, quantum):
    t = (min(pref, n) // quantum) * quantum
    while t >= quantum:
        if n % t == 0:
            return t
        t -= quantum
    return n


def _dot(a, b):
    return jnp.dot(a.astype(BF16), b.astype(BF16), preferred_element_type=F32)


def _dot_nt(a, b):
    return lax.dot_general(a.astype(BF16), b.astype(BF16), NT_DIMS, preferred_element_type=F32)


def _dot_tn(a, b):
    return lax.dot_general(a.astype(BF16), b.astype(BF16), TN_DIMS, preferred_element_type=F32)


def _split3(x):
    hi = x.astype(BF16)
    r1 = x - hi.astype(F32)
    mid = r1.astype(BF16)
    lo = (r1 - mid.astype(F32)).astype(BF16)
    return hi, mid, lo


def _cumsum_rows(x, tri_bf):
    hi, mid, lo = _split3(x)
    n = x.shape[1]
    cat = jnp.concatenate([hi, mid, lo], axis=1)
    s = jnp.dot(tri_bf, cat, preferred_element_type=F32)
    return s[:, :n] + s[:, n:2 * n] + s[:, 2 * n:]


def _softplus(x):
    return jnp.maximum(x, 0.0) + jnp.log1p(jnp.exp(-jnp.abs(x)))


def _silu(x):
    return x * jax.nn.sigmoid(x)


def _tri_masks(c):
    row = lax.broadcasted_iota(jnp.int32, (c, c), 0)
    col = lax.broadcasted_iota(jnp.int32, (c, c), 1)
    mask = row >= col
    return mask, jnp.where(mask, 1.0, 0.0).astype(BF16)


def _rms_kernel(x_ref, g_ref, o_ref):
    x = x_ref[...]
    ms = jnp.mean(x * x, axis=-1, keepdims=True)
    o_ref[...] = (x * lax.rsqrt(ms + RMS_EPS) * g_ref[...]).astype(o_ref.dtype)


def _rmsnorm(x, g, out_dtype):
    n, d = x.shape
    tm = _tile(n, 256, V7X_SUBLANES)
    return pl.pallas_call(
        _rms_kernel,
        grid=(n // tm,),
        in_specs=[pl.BlockSpec((tm, d), lambda i: (i, 0)),
                  pl.BlockSpec((1, d), lambda i: (0, 0))],
        out_specs=pl.BlockSpec((tm, d), lambda i: (i, 0)),
        out_shape=jax.ShapeDtypeStruct((n, d), out_dtype),
        compiler_params=_cparams("parallel"),
        name="rmsnorm",
    )(x, g.reshape(1, d))


def _mm_kernel(*refs, has_res):
    if has_res:
        x_ref, w_ref, r_ref, o_ref, wb_ref = refs
    else:
        x_ref, w_ref, o_ref, wb_ref = refs
        r_ref = None

    @pl.when(pl.program_id(1) == 0)
    def _():
        wb_ref[...] = w_ref[...].astype(BF16)

    acc = jnp.dot(x_ref[...], wb_ref[...], preferred_element_type=F32)
    if r_ref is not None:
        acc = acc + r_ref[...]
    o_ref[...] = acc.astype(o_ref.dtype)


def _matmul(x, w, *, col0=0, ncols=None, res=None, out_dtype=F32, tm=512, tn=512):
    n, k = x.shape
    ncols = w.shape[1] - col0 if ncols is None else ncols
    tm = _tile(n, tm, V7X_SUBLANES)
    tn = _tile(math.gcd(ncols, col0) if col0 else ncols, tn, V7X_LANES)
    assert ncols % tn == 0 and col0 % tn == 0
    jb = col0 // tn
    in_specs = [pl.BlockSpec((tm, k), lambda j, i: (i, 0)),
                pl.BlockSpec((k, tn), lambda j, i: (0, j + jb))]
    args = [x, w]
    if res is not None:
        in_specs.append(pl.BlockSpec((tm, tn), lambda j, i: (i, j)))
        args.append(res)
    return pl.pallas_call(
        functools.partial(_mm_kernel, has_res=res is not None),
        grid=(ncols // tn, n // tm),
        in_specs=in_specs,
        out_specs=pl.BlockSpec((tm, tn), lambda j, i: (i, j)),
        out_shape=jax.ShapeDtypeStruct((n, ncols), out_dtype),
        scratch_shapes=[pltpu.VMEM((k, tn), BF16)],
        compiler_params=_cparams("arbitrary", "arbitrary"),
        name="matmul",
    )(*args)


def _mmt_kernel(x_ref, wt_hbm, o_ref, wland, wb_ref, sem, *, row0, tn):
    j = pl.program_id(0)

    def tile_copy(jj):
        rows = pl.ds(pl.multiple_of(row0 + jj * tn, V7X_SUBLANES), tn)
        return pltpu.make_async_copy(wt_hbm.at[rows, :], wland, sem)

    @pl.when(pl.program_id(1) == 0)
    def _():
        @pl.when(j == 0)
        def _():
            tile_copy(0).start()

        tile_copy(j).wait()
        wb_ref[...] = wland[...].astype(BF16)

        @pl.when(j + 1 < pl.num_programs(0))
        def _():
            tile_copy(j + 1).start()

    o_ref[...] = lax.dot_general(x_ref[...], wb_ref[...], NT_DIMS, preferred_element_type=F32)


def _matmul_t(x, wt, *, row0, nrows, tm=512, tn=512):
    n, k = x.shape
    tm = _tile(n, tm, V7X_SUBLANES)
    tn = _tile(nrows, tn, V7X_LANES)
    assert row0 % V7X_SUBLANES == 0 and nrows % tn == 0
    return pl.pallas_call(
        functools.partial(_mmt_kernel, row0=row0, tn=tn),
        grid=(nrows // tn, n // tm),
        in_specs=[pl.BlockSpec((tm, k), lambda j, i: (i, 0)),
                  pl.BlockSpec(memory_space=pl.ANY)],
        out_specs=pl.BlockSpec((tm, tn), lambda j, i: (i, j)),
        out_shape=jax.ShapeDtypeStruct((n, nrows), F32),
        scratch_shapes=[pltpu.VMEM((tn, k), F32), pltpu.VMEM((tn, k), BF16),
                        pltpu.SemaphoreType.DMA(())],
        compiler_params=_cparams("arbitrary", "arbitrary"),
        name="matmul_t",
    )(x, wt)


def _merge_kernel(o_ref, y_ref, wg_ref, ws_ref, ga_ref, gb_ref, out_ref, wgb_ref, wsb_ref):
    @pl.when(pl.program_id(1) == 0)
    def _():
        wgb_ref[...] = wg_ref[...].astype(BF16)
        wsb_ref[...] = ws_ref[...].astype(BF16)

    pg = jnp.dot(o_ref[...], wgb_ref[...], preferred_element_type=F32)
    ps = jnp.dot(y_ref[...], wsb_ref[...], preferred_element_type=F32)
    out = jax.nn.sigmoid(ga_ref[...]) * pg + jax.nn.sigmoid(gb_ref[...]) * ps
    out_ref[...] = out.astype(out_ref.dtype)


def _merge(o_gla, y_ssd, w_pg, w_ps, gates, tm=512, tn=256):
    n, k = o_gla.shape
    d = w_pg.shape[1]
    tm = _tile(n, tm, V7X_SUBLANES)
    tn = _tile(d, tn, V7X_LANES)
    nj = d // tn
    return pl.pallas_call(
        _merge_kernel,
        grid=(nj, n // tm),
        in_specs=[pl.BlockSpec((tm, k), lambda j, i: (i, 0)),
                  pl.BlockSpec((tm, k), lambda j, i: (i, 0)),
                  pl.BlockSpec((k, tn), lambda j, i: (0, j)),
                  pl.BlockSpec((k, tn), lambda j, i: (0, j)),
                  pl.BlockSpec((tm, tn), lambda j, i: (i, j)),
                  pl.BlockSpec((tm, tn), lambda j, i: (i, j + nj))],
        out_specs=pl.BlockSpec((tm, tn), lambda j, i: (i, j)),
        out_shape=jax.ShapeDtypeStruct((n, d), BF16),
        scratch_shapes=[pltpu.VMEM((k, tn), BF16), pltpu.VMEM((k, tn), BF16)],
        compiler_params=_cparams("arbitrary", "arbitrary"),
        name="merge",
    )(o_gla, y_ssd, w_pg, w_ps, gates, gates)


def _gla_kernel(*refs, heads, dk, dv, c, t_valid, lr_off, rank, has_s0):
    if has_s0:
        q_ref, k_ref, v_ref, r_ref, lr_ref, gw_ref, gb_ref, gn_ref, s0_ref, o_ref, so_ref, s_scr = refs
    else:
        q_ref, k_ref, v_ref, r_ref, lr_ref, gw_ref, gb_ref, gn_ref, o_ref, so_ref, s_scr = refs
        s0_ref = None
    ci = pl.program_id(1)

    @pl.when(ci == 0)
    def _():
        if s0_ref is None:
            s_scr[...] = jnp.zeros_like(s_scr)
        else:
            s_scr[...] = s0_ref[...]

    mask, tri_bf = _tri_masks(c)
    lr = lr_ref[:, lr_off:lr_off + rank]
    scale = dk ** -0.5
    if t_valid < c:
        live = lax.broadcasted_iota(jnp.int32, (c, dk), 0) < t_valid
    gn = gn_ref[...]
    for h in range(heads):
        ks = slice(h * dk, (h + 1) * dk)
        vs = slice(h * dv, (h + 1) * dv)
        q = q_ref[:, ks] * scale
        k = k_ref[:, ks]
        v = v_ref[:, vs]
        zg = jnp.dot(lr, gw_ref[:, ks], preferred_element_type=F32) + gb_ref[:, ks]
        la = (jnp.minimum(zg, 0.0) - jnp.log1p(jnp.exp(-jnp.abs(zg)))) * (1.0 / GLA_GATE_TAU)
        if t_valid < c:
            la = jnp.where(live, la, 0.0)
        b = _cumsum_rows(la, tri_bf)
        qe = q * jnp.exp(b)
        ke = k * jnp.exp(-b)
        att = jnp.where(mask, _dot_nt(qe, ke), 0.0)
        s_old = s_scr[h]
        o = _dot(att, v) + _dot(qe, s_old)
        b_last = b[c - 1:c, :]
        kd = k * jnp.exp(b_last - b)
        e_col = jnp.broadcast_to(jnp.exp(b_last), (dk, dk)).T
        e_full = jnp.concatenate([e_col] * (dv // dk), axis=1)
        s_scr[h] = e_full * s_old + _dot_tn(kd, v)
        ms = jnp.mean(o * o, axis=-1, keepdims=True)
        og = o * lax.rsqrt(ms + RMS_EPS) * gn * _silu(r_ref[:, vs])
        o_ref[:, vs] = og.astype(o_ref.dtype)

    @pl.when(ci == pl.num_programs(1) - 1)
    def _():
        so_ref[...] = s_scr[...]


def _gla_call(srcs, gate_w, gate_b, gnorm, s0, *, heads, batch, n_chunks, c, t_valid, lr_off, out_rows):
    rank, kw = gate_w.shape
    dk = kw // heads
    dv = gnorm.shape[-1]
    vw = heads * dv

    def spec(name):
        arr, width, cb = srcs[name]
        if arr.ndim == 3:
            return pl.BlockSpec((None, c, width), lambda b, ci: (b, ci, cb))
        return pl.BlockSpec((c, width), lambda b, ci: (b * n_chunks + ci, cb))

    names = ("q", "k", "v", "r", "lr")
    in_specs = [spec(nm) for nm in names] + [
        pl.BlockSpec((rank, kw), lambda b, ci: (0, 0)),
        pl.BlockSpec((1, kw), lambda b, ci: (0, 0)),
        pl.BlockSpec((1, dv), lambda b, ci: (0, 0)),
    ]
    args = [srcs[nm][0] for nm in names] + [gate_w, gate_b.reshape(1, kw), gnorm.reshape(1, dv)]
    if s0 is not None:
        in_specs.append(pl.BlockSpec((None, heads, dk, dv), lambda b, ci: (b, 0, 0, 0)))
        args.append(s0)
    three_d = srcs["q"][0].ndim == 3
    if three_d:
        o_spec = pl.BlockSpec((None, c, vw), lambda b, ci: (b, ci, 0))
        o_shape = jax.ShapeDtypeStruct((batch, n_chunks * c, vw), BF16)
    else:
        o_spec = pl.BlockSpec((c, vw), lambda b, ci: (b * n_chunks + ci, 0))
        o_shape = jax.ShapeDtypeStruct((out_rows, vw), BF16)
    return pl.pallas_call(
        functools.partial(_gla_kernel, heads=heads, dk=dk, dv=dv, c=c, t_valid=t_valid,
                          lr_off=lr_off, rank=rank, has_s0=s0 is not None),
        grid=(batch, n_chunks),
        in_specs=in_specs,
        out_specs=[o_spec, pl.BlockSpec((None, heads, dk, dv), lambda b, ci: (b, 0, 0, 0))],
        out_shape=[o_shape, jax.ShapeDtypeStruct((batch, heads, dk, dv), F32)],
        scratch_shapes=[pltpu.VMEM((heads, dk, dv), F32)],
        compiler_params=_cparams("parallel", "arbitrary"),
        name="gla",
    )(*args)


def _ssd_kernel(*refs, heads, groups, p, n, c, t_valid, dt_off, conv_w, has_state):
    if has_state:
        (x_ref, bc_ref, z_ref, dt_ref, cw_ref, cb_ref, dtb_ref, alog_ref, dexp_ref, nrm_ref,
         cs_ref, h0_ref, y_ref, ho_ref, cso_ref, h_scr, xf_scr, bcf_scr, y_scr) = refs
    else:
        (x_ref, bc_ref, z_ref, dt_ref, cw_ref, cb_ref, dtb_ref, alog_ref, dexp_ref, nrm_ref,
         y_ref, ho_ref, cso_ref, h_scr, xf_scr, bcf_scr, y_scr) = refs
        cs_ref = h0_ref = None
    ci = pl.program_id(1)
    inner = heads * p
    gn = groups * n
    r_per_g = heads // groups
    pad = V7X_SUBLANES
    hist = conv_w - 1

    @pl.when(ci == 0)
    def _():
        if has_state:
            h_scr[...] = h0_ref[...]
            xf_scr[0:pad, :] = jnp.zeros((pad, inner), F32)
            bcf_scr[0:pad, :] = jnp.zeros((pad, 2 * gn), F32)
            xf_scr[pad - hist:pad, :] = cs_ref[:, 0:inner]
            bcf_scr[pad - hist:pad, :] = cs_ref[:, inner:inner + 2 * gn]
        else:
            h_scr[...] = jnp.zeros_like(h_scr)
            xf_scr[0:pad, :] = jnp.zeros((pad, inner), F32)
            bcf_scr[0:pad, :] = jnp.zeros((pad, 2 * gn), F32)

    xf_scr[pad:pad + c, :] = x_ref[...]
    bcf_scr[pad:pad + c, :] = bc_ref[...]
    xc = cb_ref[:, 0:inner] + xf_scr[pad - hist:pad - hist + c, :] * cw_ref[0:1, 0:inner]
    bcc = cb_ref[:, inner:] + bcf_scr[pad - hist:pad - hist + c, :] * cw_ref[0:1, inner:]
    for j in range(1, conv_w):
        xc = xc + xf_scr[pad - hist + j:pad - hist + j + c, :] * cw_ref[j:j + 1, 0:inner]
        bcc = bcc + bcf_scr[pad - hist + j:pad - hist + j + c, :] * cw_ref[j:j + 1, inner:]
    xs = _silu(xc)
    bcs = _silu(bcc)
    xf_scr[0:pad, :] = x_ref[c - pad:c, :]
    bcf_scr[0:pad, :] = bc_ref[c - pad:c, :]

    dt = _softplus(dt_ref[:, dt_off:dt_off + heads] + dtb_ref[...])
    if t_valid < c:
        dt = jnp.where(lax.broadcasted_iota(jnp.int32, (c, heads), 0) < t_valid, dt, 0.0)
    a_neg = -jnp.exp(alog_ref[...])
    mask, tri_bf = _tri_masks(c)
    cum = _cumsum_rows(dt * a_neg, tri_bf)
    sq = V7X_LANES
    both = jnp.concatenate([cum, dt], axis=1)
    if both.shape[1] < sq:
        both = jnp.concatenate([both, jnp.zeros((c, sq - both.shape[1]), F32)], axis=1)
    if c < sq:
        both = jnp.concatenate([both, jnp.zeros((sq - c, sq), F32)], axis=0)
    both_t = both.T
    e_cum = jnp.exp(cum)
    last = cum[c - 1:c, :]
    wk = jnp.exp(last - cum) * dt
    e_last = jnp.exp(last)

    for g in range(groups):
        bg = bcs[:, g * n:(g + 1) * n]
        cg = bcs[:, gn + g * n:gn + (g + 1) * n]
        cbm = _dot_nt(cg, bg)
        hg = h_scr[g]
        ych = _dot_nt(cg, hg)
        xw_parts = []
        for r in range(r_per_g):
            hd = g * r_per_g + r
            cum_col = cum[:, hd:hd + 1]
            cum_row = both_t[hd:hd + 1, 0:c]
            dt_row = both_t[heads + hd:heads + hd + 1, 0:c]
            decay = jnp.exp(jnp.where(mask, cum_col - cum_row, -jnp.inf))
            w = cbm * decay * dt_row
            xh = xs[:, hd * p:(hd + 1) * p]
            yh = _dot(w, xh) + ych[:, r * p:(r + 1) * p] * e_cum[:, hd:hd + 1]
            y_scr[:, hd * p:(hd + 1) * p] = yh
            xw_parts.append(xh * wk[:, hd:hd + 1])
        xw = jnp.concatenate(xw_parts, axis=1)
        upd = _dot_tn(xw, bg)
        for r in range(r_per_g):
            hd = g * r_per_g + r
            rs = slice(r * p, (r + 1) * p)
            h_scr[g, rs, :] = e_last[:, hd:hd + 1] * hg[rs, :] + upd[rs, :]

    y = (y_scr[...] + dexp_ref[...] * xs) * _silu(z_ref[...])
    gw = inner // groups
    for g in range(groups):
        gs = slice(g * gw, (g + 1) * gw)
        yg = y[:, gs]
        ms = jnp.mean(yg * yg, axis=-1, keepdims=True)
        y_ref[:, gs] = (yg * lax.rsqrt(ms + RMS_EPS) * nrm_ref[:, gs]).astype(y_ref.dtype)

    @pl.when(ci == pl.num_programs(1) - 1)
    def _():
        ho_ref[...] = h_scr[...]
        cso_ref[:, 0:inner] = x_ref[t_valid - hist:t_valid, :]
        cso_ref[:, inner:] = bc_ref[t_valid - hist:t_valid, :]


def _ssd_call(srcs, conv_w, conv_b, dt_bias, a_log, d_skip, ssd_norm, conv_state, h0, *,
              heads, groups, n, batch, n_chunks, c, t_valid, dt_off, out_rows):
    kw, conv_dim = conv_w.shape
    inner = ssd_norm.shape[-1]
    p = inner // heads
    gn = groups * n
    r_per_g = heads // groups

    def spec(name):
        arr, width, cb = srcs[name]
        if arr.ndim == 3:
            return pl.BlockSpec((None, c, width), lambda b, ci: (b, ci, cb))
        return pl.BlockSpec((c, width), lambda b, ci: (b * n_chunks + ci, cb))

    def full(shape):
        return pl.BlockSpec(shape, lambda b, ci: (0,) * len(shape))

    names = ("x", "bc", "z", "dt")
    in_specs = [spec(nm) for nm in names] + [
        full((kw, conv_dim)), full((1, conv_dim)), full((1, heads)), full((1, heads)),
        full((1, inner)), full((1, inner))]
    args = [srcs[nm][0] for nm in names] + [
        conv_w, conv_b.reshape(1, conv_dim), dt_bias.reshape(1, heads), a_log.reshape(1, heads),
        jnp.repeat(d_skip, p).reshape(1, inner), ssd_norm.reshape(1, inner)]
    has_state = h0 is not None
    if has_state:
        in_specs += [pl.BlockSpec((None, kw - 1, conv_dim), lambda b, ci: (b, 0, 0)),
                     pl.BlockSpec((None, groups, r_per_g * p, n), lambda b, ci: (b, 0, 0, 0))]
        args += [conv_state, h0.reshape(batch, groups, r_per_g * p, n)]
    three_d = srcs["x"][0].ndim == 3
    if three_d:
        y_spec = pl.BlockSpec((None, c, inner), lambda b, ci: (b, ci, 0))
        y_shape = jax.ShapeDtypeStruct((batch, n_chunks * c, inner), BF16)
    else:
        y_spec = pl.BlockSpec((c, inner), lambda b, ci: (b * n_chunks + ci, 0))
        y_shape = jax.ShapeDtypeStruct((out_rows, inner), BF16)
    assert t_valid >= kw - 1, "the new conv state must lie inside the last chunk"
    y, h_new, conv_new = pl.pallas_call(
        functools.partial(_ssd_kernel, heads=heads, groups=groups, p=p, n=n, c=c, t_valid=t_valid,
                          dt_off=dt_off, conv_w=kw, has_state=has_state),
        grid=(batch, n_chunks),
        in_specs=in_specs,
        out_specs=[y_spec, pl.BlockSpec((None, groups, r_per_g * p, n), lambda b, ci: (b, 0, 0, 0)),
                   pl.BlockSpec((None, kw - 1, conv_dim), lambda b, ci: (b, 0, 0))],
        out_shape=[y_shape, jax.ShapeDtypeStruct((batch, groups, r_per_g * p, n), F32),
                   jax.ShapeDtypeStruct((batch, kw - 1, conv_dim), F32)],
        scratch_shapes=[pltpu.VMEM((groups, r_per_g * p, n), F32),
                        pltpu.VMEM((V7X_SUBLANES + c, inner), F32),
                        pltpu.VMEM((V7X_SUBLANES + c, 2 * gn), F32),
                        pltpu.VMEM((c, inner), F32)],
        compiler_params=_cparams("parallel", "arbitrary"),
        name="ssd",
    )(*args)
    return y, h_new.reshape(batch, heads, p, n), conv_new


def _attn_kernel(q_ref, k_ref, v_ref, o_ref, *, heads, hd):
    scale = hd ** -0.5
    for h in range(heads):
        hs = slice(h * hd, (h + 1) * hd)
        s = _dot_nt(q_ref[:, hs], k_ref[:, hs]) * scale
        m = jnp.max(s, axis=-1, keepdims=True)
        e = jnp.exp(s - m)
        prob = e / jnp.sum(e, axis=-1, keepdims=True)
        o_ref[:, hs] = _dot(prob, v_ref[:, hs]).astype(o_ref.dtype)


def _attention(q, mem_k, mem_v, *, heads, tq=512):
    batch, t, w = q.shape
    m = mem_k.shape[1]
    tq = _tile(t, tq, V7X_SUBLANES)
    return pl.pallas_call(
        functools.partial(_attn_kernel, heads=heads, hd=w // heads),
        grid=(batch, t // tq),
        in_specs=[pl.BlockSpec((None, tq, w), lambda b, i: (b, i, 0)),
                  pl.BlockSpec((None, m, w), lambda b, i: (b, 0, 0)),
                  pl.BlockSpec((None, m, w), lambda b, i: (b, 0, 0))],
        out_specs=pl.BlockSpec((None, tq, w), lambda b, i: (b, i, 0)),
        out_shape=jax.ShapeDtypeStruct((batch, t, w), BF16),
        compiler_params=_cparams("parallel", "arbitrary"),
        name="cross_attention",
    )(q, mem_k, mem_v)


def _router_kernel(x_ref, g_ref, rw_ref, rb_ref, h_ref, idx_ref, gate_ref, *, n_exp, top_k):
    x = x_ref[...]
    ms = jnp.mean(x * x, axis=-1, keepdims=True)
    h = x * lax.rsqrt(ms + RMS_EPS) * g_ref[...]
    tm, d = x.shape
    nseg = d // V7X_LANES
    for s in range(nseg):
        h_ref[pl.ds(s, tm, stride=nseg), :] = h[:, s * V7X_LANES:(s + 1) * V7X_LANES]
    h_hi, h_mid, h_lo = _split3(h)
    w_hi, w_mid, w_lo = _split3(rw_ref[...])

    def d(a, b):
        return jnp.dot(a, b, preferred_element_type=F32)

    logits = (d(h_hi, w_hi) + (d(h_hi, w_mid) + d(h_mid, w_hi))
              + (d(h_hi, w_lo) + d(h_mid, w_mid) + d(h_lo, w_hi))) + rb_ref[...]
    lanes = lax.broadcasted_iota(jnp.int32, logits.shape, 1).astype(F32)
    work = logits
    vals, idxs = [], []
    for _ in range(top_k):
        mx = jnp.max(work, axis=-1, keepdims=True)
        ix = jnp.min(jnp.where(work == mx, lanes, float(n_exp)), axis=-1, keepdims=True)
        vals.append(mx)
        idxs.append(ix)
        work = jnp.where(lanes == ix, -jnp.inf, work)
    es = [jnp.exp(v - vals[0]) for v in vals]
    tot = es[0]
    for e in es[1:]:
        tot = tot + e
    out_lanes = lax.broadcasted_iota(jnp.int32, idx_ref.shape, 1)
    idx_out = jnp.zeros(idx_ref.shape, jnp.int32)
    gate_out = jnp.zeros(gate_ref.shape, F32)
    for j in range(top_k):
        idx_out = jnp.where(out_lanes == j, idxs[j].astype(jnp.int32), idx_out)
        gate_out = jnp.where(out_lanes == j, es[j] / tot, gate_out)
    idx_ref[...] = idx_out
    gate_ref[...] = gate_out


def _router(x, g, router_w, router_b):
    n, d = x.shape
    n_exp = router_w.shape[1]
    tm = _tile(n, 256, V7X_SUBLANES)
    nseg = d // V7X_LANES
    return pl.pallas_call(
        functools.partial(_router_kernel, n_exp=n_exp, top_k=TOP_K),
        grid=(n // tm,),
        in_specs=[pl.BlockSpec((tm, d), lambda i: (i, 0)),
                  pl.BlockSpec((1, d), lambda i: (0, 0)),
                  pl.BlockSpec((d, n_exp), lambda i: (0, 0)),
                  pl.BlockSpec((1, n_exp), lambda i: (0, 0))],
        out_specs=[pl.BlockSpec((tm * nseg, V7X_LANES), lambda i: (i, 0)),
                   pl.BlockSpec((tm, V7X_LANES), lambda i: (i, 0)),
                   pl.BlockSpec((tm, V7X_LANES), lambda i: (i, 0))],
        out_shape=[jax.ShapeDtypeStruct((n * nseg, V7X_LANES), F32),
                   jax.ShapeDtypeStruct((n, V7X_LANES), jnp.int32),
                   jax.ShapeDtypeStruct((n, V7X_LANES), F32)],
        compiler_params=_cparams("parallel"),
        name="router",
    )(x, g.reshape(1, d), router_w, router_b.reshape(1, n_exp))


def _row_copy(src_hbm, row, dst, slot, sem):
    return pltpu.make_async_copy(src_hbm.at[pl.ds(row, 1), :], dst.at[pl.ds(slot, 1), :], sem)


def _token_copy(h_hbm, tok, buf, slot, sem, nseg):
    src = h_hbm.at[pl.ds(pl.multiple_of(tok * nseg, nseg), nseg), :]
    dst = buf.at[pl.ds(pl.multiple_of(slot * nseg, nseg), nseg), :]
    return pltpu.make_async_copy(src, dst, sem)


def _gather_kernel(nrows_ref, idx_ref, h_hbm, o_ref, buf, sem, *, tg, nseg):
    @pl.when(pl.program_id(0) * tg < nrows_ref[0])
    def _():
        def start(r, carry):
            _token_copy(h_hbm, idx_ref[0, 0, r], buf, r, sem, nseg).start()
            return carry

        lax.fori_loop(0, tg, start, 0)

        def wait(r, carry):
            _token_copy(h_hbm, 0, buf, r, sem, nseg).wait()
            return carry

        lax.fori_loop(0, tg, wait, 0)
        for s in range(nseg):
            seg = buf[pl.ds(s, tg, stride=nseg), :]
            o_ref[:, s * V7X_LANES:(s + 1) * V7X_LANES] = seg.astype(o_ref.dtype)


def _gather_rows(h_tok, src, n_rows, d, tg=256):
    m = src.shape[0]
    nseg = d // V7X_LANES
    return pl.pallas_call(
        functools.partial(_gather_kernel, tg=tg, nseg=nseg),
        grid_spec=pltpu.PrefetchScalarGridSpec(
            num_scalar_prefetch=1,
            grid=(m // tg,),
            in_specs=[pl.BlockSpec((1, 1, tg), lambda i, nr: (i, 0, 0), memory_space=pltpu.SMEM),
                      pl.BlockSpec(memory_space=pl.ANY)],
            out_specs=pl.BlockSpec((tg, d), lambda i, nr: (i, 0)),
            scratch_shapes=[pltpu.VMEM((tg * nseg, V7X_LANES), F32), pltpu.SemaphoreType.DMA(())]),
        out_shape=jax.ShapeDtypeStruct((m, d), BF16),
        compiler_params=_cparams("arbitrary"),
        name="moe_gather",
    )(n_rows, src.reshape(m // tg, 1, tg), h_tok)


MOE_SUB_ROWS = 128


def _expert_weights(be_ref, nxt_ref, cnt_ref, copies, cast):
    j, i = pl.program_id(0), pl.program_id(1)
    first = jnp.logical_and(cnt_ref[i] > 0,
                            jnp.logical_or(i == 0, be_ref[i] != be_ref[jnp.maximum(i - 1, 0)]))

    @pl.when(first)
    def _():
        @pl.when(jnp.logical_and(j == 0, i == 0))
        def _():
            for cp in copies(be_ref[0], 0):
                cp.start()

        for cp in copies(be_ref[i], j):
            cp.wait()
        cast()
        nxt = nxt_ref[i]

        @pl.when(nxt >= 0)
        def _():
            for cp in copies(nxt, j):
                cp.start()

        @pl.when(jnp.logical_and(nxt < 0, j + 1 < pl.num_programs(0)))
        def _():
            for cp in copies(be_ref[0], j + 1):
                cp.start()


def _ffn1_kernel(be_ref, nu_ref, nxt_ref, cnt_ref, x_ref, w1_hbm, bg_ref, bu_ref, o_ref,
                 wland, wgb_ref, wub_ref, sem, *, ff, tf):
    def copies(e, jj):
        return [pltpu.make_async_copy(w1_hbm.at[e, :, pl.ds(pl.multiple_of(half * ff + jj * tf, tf), tf)],
                                      wland.at[half], sem.at[half]) for half in range(2)]

    def cast():
        wgb_ref[...] = wland[0].astype(BF16)
        wub_ref[...] = wland[1].astype(BF16)

    _expert_weights(be_ref, nxt_ref, cnt_ref, copies, cast)
    i = pl.program_id(1)
    for s in range(x_ref.shape[0] // MOE_SUB_ROWS):
        @pl.when(s * MOE_SUB_ROWS < cnt_ref[i])
        def _():
            rs = slice(s * MOE_SUB_ROWS, (s + 1) * MOE_SUB_ROWS)
            x = x_ref[rs, :]
            g_ = jnp.dot(x, wgb_ref[...], preferred_element_type=F32) + bg_ref[...]
            u_ = jnp.dot(x, wub_ref[...], preferred_element_type=F32) + bu_ref[...]
            g_ = jnp.minimum(g_, SWIGLU_LIMIT)
            u_ = jnp.clip(u_, -SWIGLU_LIMIT, SWIGLU_LIMIT)
            act = (u_ + 1.0) * g_ * jax.nn.sigmoid(SWIGLU_ALPHA * g_)
            o_ref[rs, :] = act.astype(o_ref.dtype)


def _ffn2_kernel(be_ref, nu_ref, nxt_ref, cnt_ref, x_ref, w2_hbm, b_ref, o_ref, wland, wb_ref, sem, *, tn):
    def copies(e, jj):
        return [pltpu.make_async_copy(w2_hbm.at[e, :, pl.ds(pl.multiple_of(jj * tn, tn), tn)], wland, sem)]

    def cast():
        wb_ref[...] = wland[...].astype(BF16)

    _expert_weights(be_ref, nxt_ref, cnt_ref, copies, cast)
    i = pl.program_id(1)
    for s in range(x_ref.shape[0] // MOE_SUB_ROWS):
        @pl.when(s * MOE_SUB_ROWS < cnt_ref[i])
        def _():
            rs = slice(s * MOE_SUB_ROWS, (s + 1) * MOE_SUB_ROWS)
            o_ref[rs, :] = jnp.dot(x_ref[rs, :], wb_ref[...], preferred_element_type=F32) + b_ref[...]


def _expert_ffn(rows, blk_e, n_used, nxt_e, cnt, w1, b1, w2, b2, *, tm, tf=512, tn=1024):
    m_pad, d = rows.shape
    n_exp, _, ff2 = w1.shape
    ff = ff2 // 2
    n_blk = m_pad // tm
    tf = _tile(ff, tf, V7X_LANES)
    tn = _tile(d, tn, V7X_LANES)
    nf = ff // tf
    assert tm % MOE_SUB_ROWS == 0

    def row_blk(i, nu):
        return jnp.minimum(i, nu[0] - 1)

    act = pl.pallas_call(
        functools.partial(_ffn1_kernel, ff=ff, tf=tf),
        grid_spec=pltpu.PrefetchScalarGridSpec(
            num_scalar_prefetch=4,
            grid=(nf, n_blk),
            in_specs=[pl.BlockSpec((tm, d), lambda j, i, be, nu, nx, ct: (row_blk(i, nu), 0)),
                      pl.BlockSpec(memory_space=pl.ANY),
                      pl.BlockSpec((None, 1, tf), lambda j, i, be, nu, nx, ct: (be[i], 0, j)),
                      pl.BlockSpec((None, 1, tf), lambda j, i, be, nu, nx, ct: (be[i], 0, j + nf))],
            out_specs=pl.BlockSpec((tm, tf), lambda j, i, be, nu, nx, ct: (row_blk(i, nu), j)),
            scratch_shapes=[pltpu.VMEM((2, d, tf), F32), pltpu.VMEM((d, tf), BF16),
                            pltpu.VMEM((d, tf), BF16), pltpu.SemaphoreType.DMA((2,))]),
        out_shape=jax.ShapeDtypeStruct((m_pad, ff), BF16),
        compiler_params=_cparams("arbitrary", "arbitrary"),
        name="moe_ffn1",
    )(blk_e, n_used, nxt_e, cnt, rows, w1, b1.reshape(n_exp, 1, ff2), b1.reshape(n_exp, 1, ff2))

    return pl.pallas_call(
        functools.partial(_ffn2_kernel, tn=tn),
        grid_spec=pltpu.PrefetchScalarGridSpec(
            num_scalar_prefetch=4,
            grid=(d // tn, n_blk),
            in_specs=[pl.BlockSpec((tm, ff), lambda j, i, be, nu, nx, ct: (row_blk(i, nu), 0)),
                      pl.BlockSpec(memory_space=pl.ANY),
                      pl.BlockSpec((None, 1, tn), lambda j, i, be, nu, nx, ct: (be[i], 0, j))],
            out_specs=pl.BlockSpec((tm, tn), lambda j, i, be, nu, nx, ct: (row_blk(i, nu), j)),
            scratch_shapes=[pltpu.VMEM((ff, tn), F32), pltpu.VMEM((ff, tn), BF16),
                            pltpu.SemaphoreType.DMA(())]),
        out_shape=jax.ShapeDtypeStruct((m_pad, d), F32),
        compiler_params=_cparams("arbitrary", "arbitrary"),
        name="moe_ffn2",
    )(blk_e, n_used, nxt_e, cnt, act, w2, b2.reshape(n_exp, 1, d))


def _combine_kernel(dest_ref, ys_hbm, gate_ref, x_ref, g_ref, o_ref, buf, sem, *, tc, top_k):
    def start(r, carry):
        for j in range(top_k):
            _row_copy(ys_hbm, dest_ref[0, 0, r * top_k + j], buf.at[j], r, sem).start()
        return carry

    lax.fori_loop(0, tc, start, 0)

    def wait(r, carry):
        for j in range(top_k):
            _row_copy(ys_hbm, 0, buf.at[j], r, sem).wait()
        return carry

    lax.fori_loop(0, tc, wait, 0)
    gate = gate_ref[...]
    y = buf[0] * gate[:, 0:1]
    for j in range(1, top_k):
        y = y + buf[j] * gate[:, j:j + 1]
    x = x_ref[...] + y
    ms = jnp.mean(x * x, axis=-1, keepdims=True)
    o_ref[...] = x * lax.rsqrt(ms + RMS_EPS) * g_ref[...]


def _combine(ys, dest, gate, x, g, tc=128):
    n, d = x.shape
    tc = _tile(n, tc, V7X_SUBLANES)
    return pl.pallas_call(
        functools.partial(_combine_kernel, tc=tc, top_k=TOP_K),
        grid=(n // tc,),
        in_specs=[pl.BlockSpec((1, 1, tc * TOP_K), lambda i: (i, 0, 0), memory_space=pltpu.SMEM),
                  pl.BlockSpec(memory_space=pl.ANY),
                  pl.BlockSpec((tc, V7X_LANES), lambda i: (i, 0)),
                  pl.BlockSpec((tc, d), lambda i: (i, 0)),
                  pl.BlockSpec((1, d), lambda i: (0, 0))],
        out_specs=pl.BlockSpec((tc, d), lambda i: (i, 0)),
        out_shape=jax.ShapeDtypeStruct((n, d), F32),
        scratch_shapes=[pltpu.VMEM((TOP_K, tc, d), F32), pltpu.SemaphoreType.DMA(())],
        compiler_params=_cparams("arbitrary"),
        name="moe_combine",
    )(dest.reshape(n // tc, 1, tc * TOP_K), ys, gate, x, g.reshape(1, d))


def _moe_layout(top_i, n_exp, tm):
    n_tok = top_i.shape[0]
    m = n_tok * TOP_K
    e_flat = top_i.reshape(-1)
    onehot = (e_flat[:, None] == jnp.arange(n_exp, dtype=jnp.int32)[None, :]).astype(jnp.int32)
    rank = jnp.sum((jnp.cumsum(onehot, axis=0) - onehot) * onehot, axis=1)
    counts = jnp.sum(onehot, axis=0)
    pad_counts = (counts + tm - 1) // tm * tm
    pad_end = jnp.cumsum(pad_counts)
    pad_start = pad_end - pad_counts
    dest = pad_start[e_flat] + rank
    n_blk = (m + n_exp * (tm - 1)) // tm + 1
    tok = jnp.arange(m, dtype=jnp.int32) // TOP_K
    src = jnp.zeros((n_blk * tm,), jnp.int32).at[dest].set(tok)
    blk_start = jnp.arange(n_blk, dtype=jnp.int32) * tm
    blk_e = jnp.minimum(jnp.sum(pad_end[None, :] <= blk_start[:, None], axis=1), n_exp - 1).astype(jnp.int32)
    n_used = (pad_end[-1] // tm).astype(jnp.int32)
    blk = jnp.arange(n_blk, dtype=jnp.int32)
    used = blk < n_used
    cnt = jnp.where(used, jnp.clip(counts[blk_e] - (blk_start - pad_start[blk_e]), 0, tm), 0).astype(jnp.int32)
    starts = jnp.logical_and(used, jnp.concatenate([jnp.ones((1,), bool), blk_e[1:] != blk_e[:-1]]))
    start_idx = jnp.where(starts, blk, n_blk)
    nxt_idx = jnp.concatenate([lax.cummin(start_idx, reverse=True)[1:], jnp.full((1,), n_blk, jnp.int32)])
    nxt_e = jnp.where(nxt_idx < n_blk, blk_e[jnp.minimum(nxt_idx, n_blk - 1)], -1).astype(jnp.int32)
    return dest.astype(jnp.int32), src, blk_e, n_used.reshape(1), nxt_e, cnt


def _pad_seq(x, t_pad):
    b, t, w = x.shape
    return x if t == t_pad else jnp.pad(x, ((0, 0), (0, t_pad - t), (0, 0)))


def kernel(x_prompt, x_sample, mem_prompt, state_gla, state_ssd, state_conv, cache_mem_k, cache_mem_v,
           norm_mix, w_in, gla_gate_w, gla_gate_b, gla_norm, ssd_conv_w, ssd_conv_b, ssd_dt_bias,
           ssd_a_log, ssd_d, ssd_norm, w_proj_gla, w_proj_ssd, w_out, norm_cross, norm_mem, w_cq,
           w_ck, w_cv, w_co, norm_ffn, router_w, router_b, moe_w1, moe_b1, moe_w2, moe_b2, norm_final):
    assert w_in.shape[0] == 1, "single layer"
    bp, tp, d = x_prompt.shape
    bs, ts, _ = x_sample.shape
    n_p, n_s = bp * tp, bs * ts
    n_all = n_p + n_s

    dv = gla_norm.shape[-1]
    val_w = w_proj_gla.shape[1]
    g_heads = val_w // dv
    key_w = gla_gate_w.shape[-1]
    rank = gla_gate_w.shape[1]
    inner = ssd_norm.shape[-1]
    s_heads = ssd_a_log.shape[-1]
    conv_dim = ssd_conv_w.shape[-1]
    n_state = state_ssd.shape[-1]
    s_groups = (conv_dim - inner) // (2 * n_state)
    kw = ssd_conv_w.shape[1]
    mem_heads, mem_hd = cache_mem_k.shape[-2:]
    mem_w = mem_heads * mem_hd
    n_exp = router_w.shape[-1]
    widths = (key_w, key_w, val_w, val_w, rank, inner, conv_dim, s_heads, d, d)
    offs = [0]
    for w_ in widths:
        offs.append(offs[-1] + w_)
    o_q, o_k, o_v, o_r, o_lr, o_z, o_xbc, o_dt, o_ga, o_gb, o_end = offs
    assert key_w == val_w // 2 and inner == d and val_w == d

    w_in_t = w_in.reshape(w_in.shape[1:]).T
    x_all = jnp.concatenate([x_prompt.reshape(n_p, d), x_sample.reshape(n_s, d)], axis=0)

    h = _rmsnorm(x_all, norm_mix[0], BF16)
    proj_a = _matmul_t(h, w_in_t, row0=0, nrows=o_lr)
    proj_b = _matmul_t(h, w_in_t, row0=o_z, nrows=o_dt - o_z)
    proj_c = _matmul_t(h, w_in_t, row0=o_ga, nrows=o_end - o_ga)
    small_wt = jnp.concatenate(
        [w_in_t[o_dt:o_ga], w_in_t[o_lr:o_z],
         jnp.zeros((V7X_LANES - s_heads - rank, d), F32)], axis=0)
    proj_s = _matmul_t(h, small_wt, row0=0, nrows=V7X_LANES)
    dt_off, lr_off = 0, s_heads

    gate_w, gate_b = gla_gate_w[0], gla_gate_b[0]
    t_pad = -(-ts // V7X_SUBLANES) * V7X_SUBLANES

    def sample_view(arr):
        return _pad_seq(arr[n_p:].reshape(bs, ts, arr.shape[1]), t_pad)

    kb = key_w
    gla_src_p = {"q": (proj_a, kb, 0), "k": (proj_a, kb, 1), "v": (proj_a, val_w, o_v // val_w),
                 "r": (proj_a, val_w, o_r // val_w), "lr": (proj_s, V7X_LANES, 0)}
    c_p = math.gcd(tp, GLA_CHUNK)
    o_p, gla_p = _gla_call(gla_src_p, gate_w, gate_b, gla_norm[0], None, heads=g_heads, batch=bp,
                           n_chunks=tp // c_p, c=c_p, t_valid=c_p, lr_off=lr_off, out_rows=n_all)
    pa_s, ps_s = sample_view(proj_a), sample_view(proj_s)
    gla_src_s = {"q": (pa_s, kb, 0), "k": (pa_s, kb, 1), "v": (pa_s, val_w, o_v // val_w),
                 "r": (pa_s, val_w, o_r // val_w), "lr": (ps_s, V7X_LANES, 0)}
    o_s, gla_s = _gla_call(gla_src_s, gate_w, gate_b, gla_norm[0], state_gla[0], heads=g_heads, batch=bs,
                           n_chunks=1, c=t_pad, t_valid=ts, lr_off=lr_off, out_rows=None)
    o_all = lax.dynamic_update_slice(o_p, o_s[:, :ts].reshape(n_s, val_w), (n_p, 0))

    bc_w = conv_dim - inner
    ssd_src_p = {"x": (proj_b, inner, 1), "bc": (proj_b, bc_w, (2 * inner) // bc_w),
                 "z": (proj_b, inner, 0), "dt": (proj_s, V7X_LANES, 0)}
    cs_p = math.gcd(tp, SSD_CHUNK)
    y_p, ssd_p, xbc_p = _ssd_call(
        ssd_src_p, ssd_conv_w[0], ssd_conv_b[0], ssd_dt_bias[0], ssd_a_log[0], ssd_d[0],
        ssd_norm[0], None, None, heads=s_heads, groups=s_groups, n=n_state, batch=bp,
        n_chunks=tp // cs_p, c=cs_p, t_valid=cs_p, dt_off=dt_off, out_rows=n_all)
    pb_s = sample_view(proj_b)
    ssd_src_s = {"x": (pb_s, inner, 1), "bc": (pb_s, bc_w, (2 * inner) // bc_w),
                 "z": (pb_s, inner, 0), "dt": (ps_s, V7X_LANES, 0)}
    y_s, ssd_s, xbc_s = _ssd_call(
        ssd_src_s, ssd_conv_w[0], ssd_conv_b[0], ssd_dt_bias[0], ssd_a_log[0], ssd_d[0],
        ssd_norm[0], state_conv[0], state_ssd[0], heads=s_heads, groups=s_groups,
        n=n_state, batch=bs, n_chunks=1, c=t_pad, t_valid=ts, dt_off=dt_off, out_rows=None)
    y_all = lax.dynamic_update_slice(y_p, y_s[:, :ts].reshape(n_s, inner), (n_p, 0))

    merged = _merge(o_all, y_all, w_proj_gla[0], w_proj_ssd[0], proj_c)
    x1 = _matmul(merged, w_out[0], res=x_all)

    h2 = _rmsnorm(x1, norm_cross[0], BF16)
    q_all = _matmul(h2, w_cq[0])
    mem = _rmsnorm(mem_prompt.reshape(-1, d), norm_mem[0], BF16)
    mk = _matmul(mem, w_ck[0])
    mv = _matmul(mem, w_cv[0])
    m_len = mem_prompt.shape[1]
    a_p = _attention(q_all[:n_p].reshape(bp, tp, mem_w), mk.reshape(bp, m_len, mem_w),
                     mv.reshape(bp, m_len, mem_w), heads=mem_heads)
    q_s = _pad_seq(q_all[n_p:].reshape(bs, ts, mem_w), t_pad)
    a_s = _attention(q_s, cache_mem_k[0].reshape(bs, -1, mem_w), cache_mem_v[0].reshape(bs, -1, mem_w),
                     heads=mem_heads)
    a_all = jnp.concatenate([a_p.reshape(n_p, mem_w), a_s[:, :ts].reshape(n_s, mem_w)], axis=0)
    x2 = _matmul(a_all, w_co[0], res=x1)

    tm_e = 512
    h3, top_pad, gate_pad = _router(x2, norm_ffn[0], router_w[0], router_b[0])
    dest, src, blk_e, n_used, nxt_e, cnt = _moe_layout(top_pad[:, :TOP_K], n_exp, tm_e)
    rows = _gather_rows(h3, src, n_used * tm_e, d)
    ys = _expert_ffn(rows, blk_e, n_used, nxt_e, cnt, moe_w1[0], moe_b1[0], moe_w2[0], moe_b2[0], tm=tm_e)
    y_fin = _combine(ys, dest, gate_pad, x2, norm_final)

    y_prompt = y_fin[:n_p].reshape(bp, tp, d)
    y_sample = y_fin[n_p:].reshape(bs, ts, d)
    return (y_prompt, y_sample, gla_p[None], ssd_p[None], xbc_p[None],
            mk.reshape(1, bp, m_len, mem_heads, mem_hd), mv.reshape(1, bp, m_len, mem_heads, mem_hd),
            gla_s[None], ssd_s[None], xbc_s[None])
```

```python
import functools
import math

import jax
import jax.numpy as jnp
from jax import lax
from jax.experimental import pallas as pl
from jax.experimental.pallas import tpu as pltpu

F32 = jnp.float32
BF16 = jnp.bfloat16

RMS_EPS = 1e-6
GLA_GATE_TAU = 16.0
GLA_CHUNK = 64
SSD_CHUNK = 64
TOP_K = 4
SWIGLU_LIMIT = 7.0
SWIGLU_ALPHA = 1.702

V7X_LANES = 128
V7X_SUBLANES = 8
V7X_VMEM_BYTES = 64 * 1024 * 1024
VMEM_LIMIT = V7X_VMEM_BYTES - 8 * 1024 * 1024

NT_DIMS = (((1,), (1,)), ((), ()))
TN_DIMS = (((0,), (0,)), ((), ()))


def _cparams(*sem):
    return pltpu.CompilerParams(dimension_semantics=sem, vmem_limit_bytes=VMEM_LIMIT)


def _tile(n, pref, quantum):
    t = (min(pref, n) // quantum) * quantum
    while t >= quantum:
        if n % t == 0:
            return t
        t -= quantum
    return n


def _dot(a, b):
    return jnp.dot(a.astype(BF16), b.astype(BF16), preferred_element_type=F32)


def _dot_nt(a, b):
    return lax.dot_general(a.astype(BF16), b.astype(BF16), NT_DIMS, preferred_element_type=F32)


def _dot_tn(a, b):
    return lax.dot_general(a.astype(BF16), b.astype(BF16), TN_DIMS, preferred_element_type=F32)


def _split3(x):
    hi = x.astype(BF16)
    r1 = x - hi.astype(F32)
    mid = r1.astype(BF16)
    lo = (r1 - mid.astype(F32)).astype(BF16)
    return hi, mid, lo


def _cumsum_rows(x, tri_bf):
    hi, mid, lo = _split3(x)
    n = x.shape[1]
    cat = jnp.concatenate([hi, mid, lo], axis=1)
    s = jnp.dot(tri_bf, cat, preferred_element_type=F32)
    return s[:, :n] + s[:, n:2 * n] + s[:, 2 * n:]


def _softplus(x):
    return jnp.maximum(x, 0.0) + jnp.log1p(jnp.exp(-jnp.abs(x)))


def _silu(x):
    return x * jax.nn.sigmoid(x)


def _tri_masks(c):
    row = lax.broadcasted_iota(jnp.int32, (c, c), 0)
    col = lax.broadcasted_iota(jnp.int32, (c, c), 1)
    mask = row >= col
    return mask, jnp.where(mask, 1.0, 0.0).astype(BF16)


def _rms_kernel(x_ref, g_ref, o_ref):
    x = x_ref[...]
    ms = jnp.mean(x * x, axis=-1, keepdims=True)
    o_ref[...] = (x * lax.rsqrt(ms + RMS_EPS) * g_ref[...]).astype(o_ref.dtype)


def _rmsnorm(x, g, out_dtype):
    n, d = x.shape
    tm = _tile(n, 256, V7X_SUBLANES)
    return pl.pallas_call(
        _rms_kernel,
        grid=(n // tm,),
        in_specs=[pl.BlockSpec((tm, d), lambda i: (i, 0)),
                  pl.BlockSpec((1, d), lambda i: (0, 0))],
        out_specs=pl.BlockSpec((tm, d), lambda i: (i, 0)),
        out_shape=jax.ShapeDtypeStruct((n, d), out_dtype),
        compiler_params=_cparams("parallel"),
        name="rmsnorm",
    )(x, g.reshape(1, d))


def _mm_kernel(*refs, has_res):
    if has_res:
        x_ref, w_ref, r_ref, o_ref, wb_ref = refs
    else:
        x_ref, w_ref, o_ref, wb_ref = refs
        r_ref = None

    @pl.when(pl.program_id(1) == 0)
    def _():
        wb_ref[...] = w_ref[...].astype(BF16)

    acc = jnp.dot(x_ref[...], wb_ref[...], preferred_element_type=F32)
    if r_ref is not None:
        acc = acc + r_ref[...]
    o_ref[...] = acc.astype(o_ref.dtype)


def _matmul(x, w, *, col0=0, ncols=None, res=None, out_dtype=F32, tm=512, tn=512):
    n, k = x.shape
    ncols = w.shape[1] - col0 if ncols is None else ncols
    tm = _tile(n, tm, V7X_SUBLANES)
    tn = _tile(math.gcd(ncols, col0) if col0 else ncols, tn, V7X_LANES)
    assert ncols % tn == 0 and col0 % tn == 0
    jb = col0 // tn
    in_specs = [pl.BlockSpec((tm, k), lambda j, i: (i, 0)),
                pl.BlockSpec((k, tn), lambda j, i: (0, j + jb))]
    args = [x, w]
    if res is not None:
        in_specs.append(pl.BlockSpec((tm, tn), lambda j, i: (i, j)))
        args.append(res)
    return pl.pallas_call(
        functools.partial(_mm_kernel, has_res=res is not None),
        grid=(ncols // tn, n // tm),
        in_specs=in_specs,
        out_specs=pl.BlockSpec((tm, tn), lambda j, i: (i, j)),
        out_shape=jax.ShapeDtypeStruct((n, ncols), out_dtype),
        scratch_shapes=[pltpu.VMEM((k, tn), BF16)],
        compiler_params=_cparams("arbitrary", "arbitrary"),
        name="matmul",
    )(*args)


def _mmt_kernel(x_ref, wt_hbm, o_ref, wland, wb_ref, sem, *, row0, tn):
    j = pl.program_id(0)

    def tile_copy(jj):
        rows = pl.ds(pl.multiple_of(row0 + jj * tn, V7X_SUBLANES), tn)
        return pltpu.make_async_copy(wt_hbm.at[rows, :], wland, sem)

    @pl.when(pl.program_id(1) == 0)
    def _():
        @pl.when(j == 0)
        def _():
            tile_copy(0).start()

        tile_copy(j).wait()
        wb_ref[...] = wland[...].astype(BF16)

        @pl.when(j + 1 < pl.num_programs(0))
        def _():
            tile_copy(j + 1).start()

    o_ref[...] = lax.dot_general(x_ref[...], wb_ref[...], NT_DIMS, preferred_element_type=F32)


def _matmul_t(x, wt, *, row0, nrows, tm=512, tn=1024):
    n, k = x.shape
    tm = _tile(n, tm, V7X_SUBLANES)
    tn = _tile(nrows, tn, V7X_LANES)
    assert row0 % V7X_SUBLANES == 0 and nrows % tn == 0
    return pl.pallas_call(
        functools.partial(_mmt_kernel, row0=row0, tn=tn),
        grid=(nrows // tn, n // tm),
        in_specs=[pl.BlockSpec((tm, k), lambda j, i: (i, 0)),
                  pl.BlockSpec(memory_space=pl.ANY)],
        out_specs=pl.BlockSpec((tm, tn), lambda j, i: (i, j)),
        out_shape=jax.ShapeDtypeStruct((n, nrows), F32),
        scratch_shapes=[pltpu.VMEM((tn, k), F32), pltpu.VMEM((tn, k), BF16),
                        pltpu.SemaphoreType.DMA(())],
        compiler_params=_cparams("arbitrary", "arbitrary"),
        name="matmul_t",
    )(x, wt)


def _merge_kernel(o_ref, y_ref, wg_ref, ws_ref, ga_ref, gb_ref, out_ref, wgb_ref, wsb_ref):
    @pl.when(pl.program_id(1) == 0)
    def _():
        wgb_ref[...] = wg_ref[...].astype(BF16)
        wsb_ref[...] = ws_ref[...].astype(BF16)

    pg = jnp.dot(o_ref[...], wgb_ref[...], preferred_element_type=F32)
    ps = jnp.dot(y_ref[...], wsb_ref[...], preferred_element_type=F32)
    out = jax.nn.sigmoid(ga_ref[...]) * pg + jax.nn.sigmoid(gb_ref[...]) * ps
    out_ref[...] = out.astype(out_ref.dtype)


def _merge(o_gla, y_ssd, w_pg, w_ps, gates, tm=512, tn=256):
    n, k = o_gla.shape
    d = w_pg.shape[1]
    tm = _tile(n, tm, V7X_SUBLANES)
    tn = _tile(d, tn, V7X_LANES)
    nj = d // tn
    return pl.pallas_call(
        _merge_kernel,
        grid=(nj, n // tm),
        in_specs=[pl.BlockSpec((tm, k), lambda j, i: (i, 0)),
                  pl.BlockSpec((tm, k), lambda j, i: (i, 0)),
                  pl.BlockSpec((k, tn), lambda j, i: (0, j)),
                  pl.BlockSpec((k, tn), lambda j, i: (0, j)),
                  pl.BlockSpec((tm, tn), lambda j, i: (i, j)),
                  pl.BlockSpec((tm, tn), lambda j, i: (i, j + nj))],
        out_specs=pl.BlockSpec((tm, tn), lambda j, i: (i, j)),
        out_shape=jax.ShapeDtypeStruct((n, d), BF16),
        scratch_shapes=[pltpu.VMEM((k, tn), BF16), pltpu.VMEM((k, tn), BF16)],
        compiler_params=_cparams("arbitrary", "arbitrary"),
        name="merge",
    )(o_gla, y_ssd, w_pg, w_ps, gates, gates)


def _gla_kernel(*refs, heads, dk, dv, c, t_valid, lr_off, rank, has_s0):
    if has_s0:
        q_ref, k_ref, v_ref, r_ref, lr_ref, gw_ref, gb_ref, gn_ref, s0_ref, o_ref, so_ref, s_scr = refs
    else:
        q_ref, k_ref, v_ref, r_ref, lr_ref, gw_ref, gb_ref, gn_ref, o_ref, so_ref, s_scr = refs
        s0_ref = None
    ci = pl.program_id(1)

    @pl.when(ci == 0)
    def _():
        if s0_ref is None:
            s_scr[...] = jnp.zeros_like(s_scr)
        else:
            s_scr[...] = s0_ref[...]

    mask, tri_bf = _tri_masks(c)
    kw = heads * dk
    scale = dk ** -0.5
    gn = gn_ref[...]
    lr = lr_ref[:, lr_off:lr_off + rank]
    zg = jnp.dot(lr, gw_ref[...], preferred_element_type=F32) + gb_ref[...]
    la = (jnp.minimum(zg, 0.0) - jnp.log1p(jnp.exp(-jnp.abs(zg)))) * (1.0 / GLA_GATE_TAU)
    if t_valid < c:
        la = jnp.where(lax.broadcasted_iota(jnp.int32, (c, kw), 0) < t_valid, la, 0.0)
    b = _cumsum_rows(la, tri_bf)
    k_all = k_ref[...]
    qe = ((q_ref[...] * scale) * jnp.exp(b)).astype(BF16)
    ke = (k_all * jnp.exp(-b)).astype(BF16)
    b_last = b[c - 1:c, :]
    kd = (k_all * jnp.exp(b_last - b)).astype(BF16)
    e_hi, e_mid, e_lo = (t.astype(F32) for t in _split3(jnp.exp(b_last)))
    row8 = lax.broadcasted_iota(jnp.int32, (V7X_SUBLANES, kw), 0)
    e3 = jnp.where(row8 == 0, e_hi, jnp.where(row8 == 1, e_mid, jnp.where(row8 == 2, e_lo, 0.0)))
    ones8 = jnp.ones((V7X_SUBLANES, dv), F32)
    atts = []
    for h in range(heads):
        ks = slice(h * dk, (h + 1) * dk)
        att = lax.dot_general(qe[:, ks], ke[:, ks], NT_DIMS, preferred_element_type=F32)
        atts.append(jnp.where(mask, att, 0.0).astype(BF16))
    for h in range(heads):
        ks = slice(h * dk, (h + 1) * dk)
        vs = slice(h * dv, (h + 1) * dv)
        v = v_ref[:, vs].astype(BF16)
        s_old = s_scr[h]
        o = (jnp.dot(atts[h], v, preferred_element_type=F32)
             + jnp.dot(qe[:, ks], s_old.astype(BF16), preferred_element_type=F32))
        e_full = lax.dot_general(e3[:, ks], ones8, TN_DIMS, preferred_element_type=F32)
        s_scr[h] = e_full * s_old + lax.dot_general(kd[:, ks], v, TN_DIMS, preferred_element_type=F32)
        ms = jnp.mean(o * o, axis=-1, keepdims=True)
        og = o * lax.rsqrt(ms + RMS_EPS) * gn * _silu(r_ref[:, vs])
        o_ref[:, vs] = og.astype(o_ref.dtype)

    @pl.when(ci == pl.num_programs(1) - 1)
    def _():
        so_ref[...] = s_scr[...]


def _gla_call(srcs, gate_w, gate_b, gnorm, s0, *, heads, batch, n_chunks, c, t_valid, lr_off, out_rows):
    rank, kw = gate_w.shape
    dk = kw // heads
    dv = gnorm.shape[-1]
    vw = heads * dv

    def spec(name):
        arr, width, cb = srcs[name]
        if arr.ndim == 3:
            return pl.BlockSpec((None, c, width), lambda b, ci: (b, ci, cb))
        return pl.BlockSpec((c, width), lambda b, ci: (b * n_chunks + ci, cb))

    names = ("q", "k", "v", "r", "lr")
    in_specs = [spec(nm) for nm in names] + [
        pl.BlockSpec((rank, kw), lambda b, ci: (0, 0)),
        pl.BlockSpec((1, kw), lambda b, ci: (0, 0)),
        pl.BlockSpec((1, dv), lambda b, ci: (0, 0)),
    ]
    args = [srcs[nm][0] for nm in names] + [gate_w, gate_b.reshape(1, kw), gnorm.reshape(1, dv)]
    if s0 is not None:
        in_specs.append(pl.BlockSpec((None, heads, dk, dv), lambda b, ci: (b, 0, 0, 0)))
        args.append(s0)
    three_d = srcs["q"][0].ndim == 3
    if three_d:
        o_spec = pl.BlockSpec((None, c, vw), lambda b, ci: (b, ci, 0))
        o_shape = jax.ShapeDtypeStruct((batch, n_chunks * c, vw), BF16)
    else:
        o_spec = pl.BlockSpec((c, vw), lambda b, ci: (b * n_chunks + ci, 0))
        o_shape = jax.ShapeDtypeStruct((out_rows, vw), BF16)
    return pl.pallas_call(
        functools.partial(_gla_kernel, heads=heads, dk=dk, dv=dv, c=c, t_valid=t_valid,
                          lr_off=lr_off, rank=rank, has_s0=s0 is not None),
        grid=(batch, n_chunks),
        in_specs=in_specs,
        out_specs=[o_spec, pl.BlockSpec((None, heads, dk, dv), lambda b, ci: (b, 0, 0, 0))],
        out_shape=[o_shape, jax.ShapeDtypeStruct((batch, heads, dk, dv), F32)],
        scratch_shapes=[pltpu.VMEM((heads, dk, dv), F32)],
        compiler_params=_cparams("parallel", "arbitrary"),
        name="gla",
    )(*args)


def _ssd_kernel(*refs, heads, groups, p, n, c, t_valid, dt_off, conv_w, has_state):
    if has_state:
        (x_ref, bc_ref, z_ref, dt_ref, cw_ref, cb_ref, dtb_ref, alog_ref, dexp_ref, nrm_ref,
         cs_ref, h0_ref, y_ref, ho_ref, cso_ref, h_scr, xf_scr, bcf_scr, y_scr) = refs
    else:
        (x_ref, bc_ref, z_ref, dt_ref, cw_ref, cb_ref, dtb_ref, alog_ref, dexp_ref, nrm_ref,
         y_ref, ho_ref, cso_ref, h_scr, xf_scr, bcf_scr, y_scr) = refs
        cs_ref = h0_ref = None
    ci = pl.program_id(1)
    inner = heads * p
    gn = groups * n
    r_per_g = heads // groups
    pad = V7X_SUBLANES
    hist = conv_w - 1

    @pl.when(ci == 0)
    def _():
        if has_state:
            h_scr[...] = h0_ref[...]
            xf_scr[0:pad, :] = jnp.zeros((pad, inner), F32)
            bcf_scr[0:pad, :] = jnp.zeros((pad, 2 * gn), F32)
            xf_scr[pad - hist:pad, :] = cs_ref[:, 0:inner]
            bcf_scr[pad - hist:pad, :] = cs_ref[:, inner:inner + 2 * gn]
        else:
            h_scr[...] = jnp.zeros_like(h_scr)
            xf_scr[0:pad, :] = jnp.zeros((pad, inner), F32)
            bcf_scr[0:pad, :] = jnp.zeros((pad, 2 * gn), F32)

    xf_scr[pad:pad + c, :] = x_ref[...]
    bcf_scr[pad:pad + c, :] = bc_ref[...]
    xc = cb_ref[:, 0:inner] + xf_scr[pad - hist:pad - hist + c, :] * cw_ref[0:1, 0:inner]
    bcc = cb_ref[:, inner:] + bcf_scr[pad - hist:pad - hist + c, :] * cw_ref[0:1, inner:]
    for j in range(1, conv_w):
        xc = xc + xf_scr[pad - hist + j:pad - hist + j + c, :] * cw_ref[j:j + 1, 0:inner]
        bcc = bcc + bcf_scr[pad - hist + j:pad - hist + j + c, :] * cw_ref[j:j + 1, inner:]
    xs = _silu(xc)
    bcs = _silu(bcc)
    xf_scr[0:pad, :] = x_ref[c - pad:c, :]
    bcf_scr[0:pad, :] = bc_ref[c - pad:c, :]

    dt = _softplus(dt_ref[:, dt_off:dt_off + heads] + dtb_ref[...])
    if t_valid < c:
        dt = jnp.where(lax.broadcasted_iota(jnp.int32, (c, heads), 0) < t_valid, dt, 0.0)
    a_neg = -jnp.exp(alog_ref[...])
    mask, tri_bf = _tri_masks(c)
    cum = _cumsum_rows(dt * a_neg, tri_bf)
    sq = V7X_LANES
    both = jnp.concatenate([cum, dt], axis=1)
    if both.shape[1] < sq:
        both = jnp.concatenate([both, jnp.zeros((c, sq - both.shape[1]), F32)], axis=1)
    if c < sq:
        both = jnp.concatenate([both, jnp.zeros((sq - c, sq), F32)], axis=0)
    both_t = both.T
    e_cum = jnp.exp(cum)
    last = cum[c - 1:c, :]
    wk = jnp.exp(last - cum) * dt
    e_last = jnp.exp(last)

    for g in range(groups):
        bg = bcs[:, g * n:(g + 1) * n]
        cg = bcs[:, gn + g * n:gn + (g + 1) * n]
        cbm = _dot_nt(cg, bg)
        hg = h_scr[g]
        ych = _dot_nt(cg, hg)
        xw_parts = []
        for r in range(r_per_g):
            hd = g * r_per_g + r
            cum_col = cum[:, hd:hd + 1]
            cum_row = both_t[hd:hd + 1, 0:c]
            dt_row = both_t[heads + hd:heads + hd + 1, 0:c]
            decay = jnp.exp(jnp.where(mask, cum_col - cum_row, -jnp.inf))
            w = cbm * decay * dt_row
            xh = xs[:, hd * p:(hd + 1) * p]
            yh = _dot(w, xh) + ych[:, r * p:(r + 1) * p] * e_cum[:, hd:hd + 1]
            y_scr[:, hd * p:(hd + 1) * p] = yh
            xw_parts.append(xh * wk[:, hd:hd + 1])
        xw = jnp.concatenate(xw_parts, axis=1)
        upd = _dot_tn(xw, bg)
        for r in range(r_per_g):
            hd = g * r_per_g + r
            rs = slice(r * p, (r + 1) * p)
            h_scr[g, rs, :] = e_last[:, hd:hd + 1] * hg[rs, :] + upd[rs, :]

    y = (y_scr[...] + dexp_ref[...] * xs) * _silu(z_ref[...])
    gw = inner // groups
    for g in range(groups):
        gs = slice(g * gw, (g + 1) * gw)
        yg = y[:, gs]
        ms = jnp.mean(yg * yg, axis=-1, keepdims=True)
        y_ref[:, gs] = (yg * lax.rsqrt(ms + RMS_EPS) * nrm_ref[:, gs]).astype(y_ref.dtype)

    @pl.when(ci == pl.num_programs(1) - 1)
    def _():
        ho_ref[...] = h_scr[...]
        cso_ref[:, 0:inner] = x_ref[t_valid - hist:t_valid, :]
        cso_ref[:, inner:] = bc_ref[t_valid - hist:t_valid, :]


def _ssd_call(srcs, conv_w, conv_b, dt_bias, a_log, d_skip, ssd_norm, conv_state, h0, *,
              heads, groups, n, batch, n_chunks, c, t_valid, dt_off, out_rows):
    kw, conv_dim = conv_w.shape
    inner = ssd_norm.shape[-1]
    p = inner // heads
    gn = groups * n
    r_per_g = heads // groups

    def spec(name):
        arr, width, cb = srcs[name]
        if arr.ndim == 3:
            return pl.BlockSpec((None, c, width), lambda b, ci: (b, ci, cb))
        return pl.BlockSpec((c, width), lambda b, ci: (b * n_chunks + ci, cb))

    def full(shape):
        return pl.BlockSpec(shape, lambda b, ci: (0,) * len(shape))

    names = ("x", "bc", "z", "dt")
    in_specs = [spec(nm) for nm in names] + [
        full((kw, conv_dim)), full((1, conv_dim)), full((1, heads)), full((1, heads)),
        full((1, inner)), full((1, inner))]
    args = [srcs[nm][0] for nm in names] + [
        conv_w, conv_b.reshape(1, conv_dim), dt_bias.reshape(1, heads), a_log.reshape(1, heads),
        jnp.repeat(d_skip, p).reshape(1, inner), ssd_norm.reshape(1, inner)]
    has_state = h0 is not None
    if has_state:
        in_specs += [pl.BlockSpec((None, kw - 1, conv_dim), lambda b, ci: (b, 0, 0)),
                     pl.BlockSpec((None, groups, r_per_g * p, n), lambda b, ci: (b, 0, 0, 0))]
        args += [conv_state, h0.reshape(batch, groups, r_per_g * p, n)]
    three_d = srcs["x"][0].ndim == 3
    if three_d:
        y_spec = pl.BlockSpec((None, c, inner), lambda b, ci: (b, ci, 0))
        y_shape = jax.ShapeDtypeStruct((batch, n_chunks * c, inner), BF16)
    else:
        y_spec = pl.BlockSpec((c, inner), lambda b, ci: (b * n_chunks + ci, 0))
        y_shape = jax.ShapeDtypeStruct((out_rows, inner), BF16)
    assert t_valid >= kw - 1, "the new conv state must lie inside the last chunk"
    y, h_new, conv_new = pl.pallas_call(
        functools.partial(_ssd_kernel, heads=heads, groups=groups, p=p, n=n, c=c, t_valid=t_valid,
                          dt_off=dt_off, conv_w=kw, has_state=has_state),
        grid=(batch, n_chunks),
        in_specs=in_specs,
        out_specs=[y_spec, pl.BlockSpec((None, groups, r_per_g * p, n), lambda b, ci: (b, 0, 0, 0)),
                   pl.BlockSpec((None, kw - 1, conv_dim), lambda b, ci: (b, 0, 0))],
        out_shape=[y_shape, jax.ShapeDtypeStruct((batch, groups, r_per_g * p, n), F32),
                   jax.ShapeDtypeStruct((batch, kw - 1, conv_dim), F32)],
        scratch_shapes=[pltpu.VMEM((groups, r_per_g * p, n), F32),
                        pltpu.VMEM((V7X_SUBLANES + c, inner), F32),
                        pltpu.VMEM((V7X_SUBLANES + c, 2 * gn), F32),
                        pltpu.VMEM((c, inner), F32)],
        compiler_params=_cparams("parallel", "arbitrary"),
        name="ssd",
    )(*args)
    return y, h_new.reshape(batch, heads, p, n), conv_new


def _attn_kernel(q_ref, k_ref, v_ref, o_ref, *, heads, hd):
    scale = hd ** -0.5
    for h in range(heads):
        hs = slice(h * hd, (h + 1) * hd)
        s = _dot_nt(q_ref[:, hs], k_ref[:, hs]) * scale
        m = jnp.max(s, axis=-1, keepdims=True)
        e = jnp.exp(s - m)
        prob = e / jnp.sum(e, axis=-1, keepdims=True)
        o_ref[:, hs] = _dot(prob, v_ref[:, hs]).astype(o_ref.dtype)


def _attention(q, mem_k, mem_v, *, heads, tq=512):
    batch, t, w = q.shape
    m = mem_k.shape[1]
    tq = _tile(t, tq, V7X_SUBLANES)
    return pl.pallas_call(
        functools.partial(_attn_kernel, heads=heads, hd=w // heads),
        grid=(batch, t // tq),
        in_specs=[pl.BlockSpec((None, tq, w), lambda b, i: (b, i, 0)),
                  pl.BlockSpec((None, m, w), lambda b, i: (b, 0, 0)),
                  pl.BlockSpec((None, m, w), lambda b, i: (b, 0, 0))],
        out_specs=pl.BlockSpec((None, tq, w), lambda b, i: (b, i, 0)),
        out_shape=jax.ShapeDtypeStruct((batch, t, w), BF16),
        compiler_params=_cparams("parallel", "arbitrary"),
        name="cross_attention",
    )(q, mem_k, mem_v)


def _router_kernel(x_ref, g_ref, rw_ref, rb_ref, h_ref, idx_ref, gate_ref, *, n_exp, top_k):
    x = x_ref[...]
    ms = jnp.mean(x * x, axis=-1, keepdims=True)
    h = x * lax.rsqrt(ms + RMS_EPS) * g_ref[...]
    tm, d = x.shape
    nseg = d // V7X_LANES
    pitch = _token_pitch(d)
    for s in range(nseg):
        h_ref[pl.ds(s, tm, stride=pitch), :] = h[:, s * V7X_LANES:(s + 1) * V7X_LANES]
    h_hi, h_mid, h_lo = _split3(h)
    w_hi, w_mid, w_lo = _split3(rw_ref[...])

    def d(a, b):
        return jnp.dot(a, b, preferred_element_type=F32)

    logits = (d(h_hi, w_hi) + (d(h_hi, w_mid) + d(h_mid, w_hi))
              + (d(h_hi, w_lo) + d(h_mid, w_mid) + d(h_lo, w_hi))) + rb_ref[...]
    lanes = lax.broadcasted_iota(jnp.int32, logits.shape, 1).astype(F32)
    work = logits
    vals, idxs = [], []
    for _ in range(top_k):
        mx = jnp.max(work, axis=-1, keepdims=True)
        ix = jnp.min(jnp.where(work == mx, lanes, float(n_exp)), axis=-1, keepdims=True)
        vals.append(mx)
        idxs.append(ix)
        work = jnp.where(lanes == ix, -jnp.inf, work)
    es = [jnp.exp(v - vals[0]) for v in vals]
    tot = es[0]
    for e in es[1:]:
        tot = tot + e
    out_lanes = lax.broadcasted_iota(jnp.int32, idx_ref.shape, 1)
    idx_out = jnp.zeros(idx_ref.shape, jnp.int32)
    gate_out = jnp.zeros(gate_ref.shape, F32)
    for j in range(top_k):
        idx_out = jnp.where(out_lanes == j, idxs[j].astype(jnp.int32), idx_out)
        gate_out = jnp.where(out_lanes == j, es[j] / tot, gate_out)
    idx_ref[...] = idx_out
    gate_ref[...] = gate_out


def _router(x, g, router_w, router_b):
    n, d = x.shape
    n_exp = router_w.shape[1]
    tm = _tile(n, 256, V7X_SUBLANES)
    nseg = _token_pitch(d)
    return pl.pallas_call(
        functools.partial(_router_kernel, n_exp=n_exp, top_k=TOP_K),
        grid=(n // tm,),
        in_specs=[pl.BlockSpec((tm, d), lambda i: (i, 0)),
                  pl.BlockSpec((1, d), lambda i: (0, 0)),
                  pl.BlockSpec((d, n_exp), lambda i: (0, 0)),
                  pl.BlockSpec((1, n_exp), lambda i: (0, 0))],
        out_specs=[pl.BlockSpec((tm * nseg, V7X_LANES), lambda i: (i, 0)),
                   pl.BlockSpec((tm, V7X_LANES), lambda i: (i, 0)),
                   pl.BlockSpec((tm, V7X_LANES), lambda i: (i, 0))],
        out_shape=[jax.ShapeDtypeStruct((n * nseg, V7X_LANES), F32),
                   jax.ShapeDtypeStruct((n, V7X_LANES), jnp.int32),
                   jax.ShapeDtypeStruct((n, V7X_LANES), F32)],
        compiler_params=_cparams("parallel"),
        name="router",
    )(x, g.reshape(1, d), router_w, router_b.reshape(1, n_exp))


def _row_copy(src_hbm, row, dst, slot, sem):
    return pltpu.make_async_copy(src_hbm.at[pl.ds(row, 1), :], dst.at[pl.ds(slot, 1), :], sem)


def _token_pitch(d):
    return d // V7X_LANES + 1


def _token_copy(h_hbm, tok, buf, slot, sem, nseg, pitch):
    src = h_hbm.at[pl.ds(tok * pitch, nseg), :]
    dst = buf.at[pl.ds(slot * pitch, nseg), :]
    return pltpu.make_async_copy(src, dst, sem)


GATHER_UNROLL = 8


def _gather_kernel(nrows_ref, idx_ref, h_hbm, o_ref, buf, sem, *, tg, nseg, pitch):
    @pl.when(pl.program_id(0) * tg < nrows_ref[0])
    def _():
        def start(r, carry):
            _token_copy(h_hbm, idx_ref[0, 0, r], buf, r, sem, nseg, pitch).start()
            return carry

        lax.fori_loop(0, tg, start, 0, unroll=GATHER_UNROLL)

        def wait(r, carry):
            _token_copy(h_hbm, 0, buf, r, sem, nseg, pitch).wait()
            return carry

        lax.fori_loop(0, tg, wait, 0, unroll=GATHER_UNROLL)
        for s in range(nseg):
            seg = buf[pl.ds(s, tg, stride=pitch), :]
            o_ref[:, s * V7X_LANES:(s + 1) * V7X_LANES] = seg.astype(o_ref.dtype)


def _gather_rows(h_tok, src, n_rows, d, tg=256):
    m = src.shape[0]
    nseg = d // V7X_LANES
    pitch = _token_pitch(d)
    return pl.pallas_call(
        functools.partial(_gather_kernel, tg=tg, nseg=nseg, pitch=pitch),
        grid_spec=pltpu.PrefetchScalarGridSpec(
            num_scalar_prefetch=1,
            grid=(m // tg,),
            in_specs=[pl.BlockSpec((1, 1, tg), lambda i, nr: (i, 0, 0), memory_space=pltpu.SMEM),
                      pl.BlockSpec(memory_space=pl.ANY)],
            out_specs=pl.BlockSpec((tg, d), lambda i, nr: (i, 0)),
            scratch_shapes=[pltpu.VMEM((tg * pitch, V7X_LANES), F32), pltpu.SemaphoreType.DMA(())]),
        out_shape=jax.ShapeDtypeStruct((m, d), BF16),
        compiler_params=_cparams("arbitrary"),
        name="moe_gather",
    )(n_rows, src.reshape(m // tg, 1, tg), h_tok)


MOE_SUB_ROWS = 128


def _expert_weights(be_ref, nxt_ref, cnt_ref, copies, cast):
    j, i = pl.program_id(0), pl.program_id(1)
    first = jnp.logical_and(cnt_ref[i] > 0,
                            jnp.logical_or(i == 0, be_ref[i] != be_ref[jnp.maximum(i - 1, 0)]))

    @pl.when(first)
    def _():
        @pl.when(jnp.logical_and(j == 0, i == 0))
        def _():
            for cp in copies(be_ref[0], 0):
                cp.start()

        for cp in copies(be_ref[i], j):
            cp.wait()
        cast()
        nxt = nxt_ref[i]

        @pl.when(nxt >= 0)
        def _():
            for cp in copies(nxt, j):
                cp.start()

        @pl.when(jnp.logical_and(nxt < 0, j + 1 < pl.num_programs(0)))
        def _():
            for cp in copies(be_ref[0], j + 1):
                cp.start()


def _ffn1_kernel(be_ref, nu_ref, nxt_ref, cnt_ref, x_ref, w1_hbm, bg_ref, bu_ref, o_ref,
                 wland, wgb_ref, wub_ref, sem, *, ff, tf):
    def copies(e, jj):
        return [pltpu.make_async_copy(w1_hbm.at[e, :, pl.ds(pl.multiple_of(half * ff + jj * tf, tf), tf)],
                                      wland.at[half], sem.at[half]) for half in range(2)]

    def cast():
        wgb_ref[...] = wland[0].astype(BF16)
        wub_ref[...] = wland[1].astype(BF16)

    _expert_weights(be_ref, nxt_ref, cnt_ref, copies, cast)
    i = pl.program_id(1)
    for s in range(x_ref.shape[0] // MOE_SUB_ROWS):
        @pl.when(s * MOE_SUB_ROWS < cnt_ref[i])
        def _():
            rs = slice(s * MOE_SUB_ROWS, (s + 1) * MOE_SUB_ROWS)
            x = x_ref[rs, :]
            g_ = jnp.dot(x, wgb_ref[...], preferred_element_type=F32) + bg_ref[...]
            u_ = jnp.dot(x, wub_ref[...], preferred_element_type=F32) + bu_ref[...]
            g_ = jnp.minimum(g_, SWIGLU_LIMIT)
            u_ = jnp.clip(u_, -SWIGLU_LIMIT, SWIGLU_LIMIT)
            act = (u_ + 1.0) * g_ * jax.nn.sigmoid(SWIGLU_ALPHA * g_)
            o_ref[rs, :] = act.astype(o_ref.dtype)


def _ffn2_kernel(be_ref, nu_ref, nxt_ref, cnt_ref, x_ref, w2_hbm, b_ref, o_ref, wland, wb_ref, sem, *, tn):
    def copies(e, jj):
        return [pltpu.make_async_copy(w2_hbm.at[e, :, pl.ds(pl.multiple_of(jj * tn, tn), tn)], wland, sem)]

    def cast():
        wb_ref[...] = wland[...].astype(BF16)

    _expert_weights(be_ref, nxt_ref, cnt_ref, copies, cast)
    i = pl.program_id(1)
    for s in range(x_ref.shape[0] // MOE_SUB_ROWS):
        @pl.when(s * MOE_SUB_ROWS < cnt_ref[i])
        def _():
            rs = slice(s * MOE_SUB_ROWS, (s + 1) * MOE_SUB_ROWS)
            o_ref[rs, :] = jnp.dot(x_ref[rs, :], wb_ref[...], preferred_element_type=F32) + b_ref[...]


def _expert_ffn(rows, blk_e, n_used, nxt_e, cnt, w1, b1, w2, b2, *, tm, tf=512, tn=1024):
    m_pad, d = rows.shape
    n_exp, _, ff2 = w1.shape
    ff = ff2 // 2
    n_blk = m_pad // tm
    tf = _tile(ff, tf, V7X_LANES)
    tn = _tile(d, tn, V7X_LANES)
    nf = ff // tf
    assert tm % MOE_SUB_ROWS == 0

    def row_blk(i, nu):
        return jnp.minimum(i, nu[0] - 1)

    act = pl.pallas_call(
        functools.partial(_ffn1_kernel, ff=ff, tf=tf),
        grid_spec=pltpu.PrefetchScalarGridSpec(
            num_scalar_prefetch=4,
            grid=(nf, n_blk),
            in_specs=[pl.BlockSpec((tm, d), lambda j, i, be, nu, nx, ct: (row_blk(i, nu), 0)),
                      pl.BlockSpec(memory_space=pl.ANY),
                      pl.BlockSpec((None, 1, tf), lambda j, i, be, nu, nx, ct: (be[i], 0, j)),
                      pl.BlockSpec((None, 1, tf), lambda j, i, be, nu, nx, ct: (be[i], 0, j + nf))],
            out_specs=pl.BlockSpec((tm, tf), lambda j, i, be, nu, nx, ct: (row_blk(i, nu), j)),
            scratch_shapes=[pltpu.VMEM((2, d, tf), F32), pltpu.VMEM((d, tf), BF16),
                            pltpu.VMEM((d, tf), BF16), pltpu.SemaphoreType.DMA((2,))]),
        out_shape=jax.ShapeDtypeStruct((m_pad, ff), BF16),
        compiler_params=_cparams("arbitrary", "arbitrary"),
        name="moe_ffn1",
    )(blk_e, n_used, nxt_e, cnt, rows, w1, b1.reshape(n_exp, 1, ff2), b1.reshape(n_exp, 1, ff2))

    return pl.pallas_call(
        functools.partial(_ffn2_kernel, tn=tn),
        grid_spec=pltpu.PrefetchScalarGridSpec(
            num_scalar_prefetch=4,
            grid=(d // tn, n_blk),
            in_specs=[pl.BlockSpec((tm, ff), lambda j, i, be, nu, nx, ct: (row_blk(i, nu), 0)),
                      pl.BlockSpec(memory_space=pl.ANY),
                      pl.BlockSpec((None, 1, tn), lambda j, i, be, nu, nx, ct: (be[i], 0, j))],
            out_specs=pl.BlockSpec((tm, tn), lambda j, i, be, nu, nx, ct: (row_blk(i, nu), j)),
            scratch_shapes=[pltpu.VMEM((ff, tn), F32), pltpu.VMEM((ff, tn), BF16),
                            pltpu.SemaphoreType.DMA(())]),
        out_shape=jax.ShapeDtypeStruct((m_pad, d), F32),
        compiler_params=_cparams("arbitrary", "arbitrary"),
        name="moe_ffn2",
    )(blk_e, n_used, nxt_e, cnt, act, w2, b2.reshape(n_exp, 1, d))


def _combine_kernel(dest_ref, ys_hbm, gate_ref, x_ref, g_ref, o_ref, buf, sem, *, tc, top_k):
    def start(r, carry):
        for j in range(top_k):
            _row_copy(ys_hbm, dest_ref[0, 0, r * top_k + j], buf.at[j], r, sem).start()
        return carry

    lax.fori_loop(0, tc, start, 0)

    def wait(r, carry):
        for j in range(top_k):
            _row_copy(ys_hbm, 0, buf.at[j], r, sem).wait()
        return carry

    lax.fori_loop(0, tc, wait, 0)
    gate = gate_ref[...]
    y = buf[0] * gate[:, 0:1]
    for j in range(1, top_k):
        y = y + buf[j] * gate[:, j:j + 1]
    x = x_ref[...] + y
    ms = jnp.mean(x * x, axis=-1, keepdims=True)
    o_ref[...] = x * lax.rsqrt(ms + RMS_EPS) * g_ref[...]


def _combine(ys, dest, gate, x, g, tc=128):
    n, d = x.shape
    tc = _tile(n, tc, V7X_SUBLANES)
    return pl.pallas_call(
        functools.partial(_combine_kernel, tc=tc, top_k=TOP_K),
        grid=(n // tc,),
        in_specs=[pl.BlockSpec((1, 1, tc * TOP_K), lambda i: (i, 0, 0), memory_space=pltpu.SMEM),
                  pl.BlockSpec(memory_space=pl.ANY),
                  pl.BlockSpec((tc, V7X_LANES), lambda i: (i, 0)),
                  pl.BlockSpec((tc, d), lambda i: (i, 0)),
                  pl.BlockSpec((1, d), lambda i: (0, 0))],
        out_specs=pl.BlockSpec((tc, d), lambda i: (i, 0)),
        out_shape=jax.ShapeDtypeStruct((n, d), F32),
        scratch_shapes=[pltpu.VMEM((TOP_K, tc, d), F32), pltpu.SemaphoreType.DMA(())],
        compiler_params=_cparams("arbitrary"),
        name="moe_combine",
    )(dest.reshape(n // tc, 1, tc * TOP_K), ys, gate, x, g.reshape(1, d))


def _moe_layout(top_i, n_exp, tm):
    n_tok = top_i.shape[0]
    m = n_tok * TOP_K
    e_flat = top_i.reshape(-1)
    onehot = (e_flat[:, None] == jnp.arange(n_exp, dtype=jnp.int32)[None, :]).astype(jnp.int32)
    rank = jnp.sum((jnp.cumsum(onehot, axis=0) - onehot) * onehot, axis=1)
    counts = jnp.sum(onehot, axis=0)
    pad_counts = (counts + tm - 1) // tm * tm
    pad_end = jnp.cumsum(pad_counts)
    pad_start = pad_end - pad_counts
    dest = pad_start[e_flat] + rank
    n_blk = (m + n_exp * (tm - 1)) // tm + 1
    tok = jnp.arange(m, dtype=jnp.int32) // TOP_K
    src = jnp.zeros((n_blk * tm,), jnp.int32).at[dest].set(tok)
    blk_start = jnp.arange(n_blk, dtype=jnp.int32) * tm
    blk_e = jnp.minimum(jnp.sum(pad_end[None, :] <= blk_start[:, None], axis=1), n_exp - 1).astype(jnp.int32)
    n_used = (pad_end[-1] // tm).astype(jnp.int32)
    blk = jnp.arange(n_blk, dtype=jnp.int32)
    used = blk < n_used
    cnt = jnp.where(used, jnp.clip(counts[blk_e] - (blk_start - pad_start[blk_e]), 0, tm), 0).astype(jnp.int32)
    starts = jnp.logical_and(used, jnp.concatenate([jnp.ones((1,), bool), blk_e[1:] != blk_e[:-1]]))
    start_idx = jnp.where(starts, blk, n_blk)
    nxt_idx = jnp.concatenate([lax.cummin(start_idx, reverse=True)[1:], jnp.full((1,), n_blk, jnp.int32)])
    nxt_e = jnp.where(nxt_idx < n_blk, blk_e[jnp.minimum(nxt_idx, n_blk - 1)], -1).astype(jnp.int32)
    return dest.astype(jnp.int32), src, blk_e, n_used.reshape(1), nxt_e, cnt


def _pad_seq(x, t_pad):
    b, t, w = x.shape
    return x if t == t_pad else jnp.pad(x, ((0, 0), (0, t_pad - t), (0, 0)))


def kernel(x_prompt, x_sample, mem_prompt, state_gla, state_ssd, state_conv, cache_mem_k, cache_mem_v,
           norm_mix, w_in, gla_gate_w, gla_gate_b, gla_norm, ssd_conv_w, ssd_conv_b, ssd_dt_bias,
           ssd_a_log, ssd_d, ssd_norm, w_proj_gla, w_proj_ssd, w_out, norm_cross, norm_mem, w_cq,
           w_ck, w_cv, w_co, norm_ffn, router_w, router_b, moe_w1, moe_b1, moe_w2, moe_b2, norm_final):
    assert w_in.shape[0] == 1, "single layer"
    bp, tp, d = x_prompt.shape
    bs, ts, _ = x_sample.shape
    n_p, n_s = bp * tp, bs * ts
    n_all = n_p + n_s

    dv = gla_norm.shape[-1]
    val_w = w_proj_gla.shape[1]
    g_heads = val_w // dv
    key_w = gla_gate_w.shape[-1]
    rank = gla_gate_w.shape[1]
    inner = ssd_norm.shape[-1]
    s_heads = ssd_a_log.shape[-1]
    conv_dim = ssd_conv_w.shape[-1]
    n_state = state_ssd.shape[-1]
    s_groups = (conv_dim - inner) // (2 * n_state)
    kw = ssd_conv_w.shape[1]
    mem_heads, mem_hd = cache_mem_k.shape[-2:]
    mem_w = mem_heads * mem_hd
    n_exp = router_w.shape[-1]
    widths = (key_w, key_w, val_w, val_w, rank, inner, conv_dim, s_heads, d, d)
    offs = [0]
    for w_ in widths:
        offs.append(offs[-1] + w_)
    o_q, o_k, o_v, o_r, o_lr, o_z, o_xbc, o_dt, o_ga, o_gb, o_end = offs
    assert key_w == val_w // 2 and inner == d and val_w == d

    w_in_t = w_in.reshape(w_in.shape[1:]).T
    x_all = jnp.concatenate([x_prompt.reshape(n_p, d), x_sample.reshape(n_s, d)], axis=0)

    h = _rmsnorm(x_all, norm_mix[0], BF16)
    proj_a = _matmul_t(h, w_in_t, row0=0, nrows=o_lr)
    proj_b = _matmul_t(h, w_in_t, row0=o_z, nrows=o_dt - o_z)
    proj_c = _matmul_t(h, w_in_t, row0=o_ga, nrows=o_end - o_ga)
    small_wt = jnp.concatenate(
        [w_in_t[o_dt:o_ga], w_in_t[o_lr:o_z],
         jnp.zeros((V7X_LANES - s_heads - rank, d), F32)], axis=0)
    proj_s = _matmul_t(h, small_wt, row0=0, nrows=V7X_LANES)
    dt_off, lr_off = 0, s_heads

    gate_w, gate_b = gla_gate_w[0], gla_gate_b[0]
    t_pad = -(-ts // V7X_SUBLANES) * V7X_SUBLANES

    def sample_view(arr):
        return _pad_seq(arr[n_p:].reshape(bs, ts, arr.shape[1]), t_pad)

    kb = key_w
    gla_src_p = {"q": (proj_a, kb, 0), "k": (proj_a, kb, 1), "v": (proj_a, val_w, o_v // val_w),
                 "r": (proj_a, val_w, o_r // val_w), "lr": (proj_s, V7X_LANES, 0)}
    c_p = math.gcd(tp, GLA_CHUNK)
    o_p, gla_p = _gla_call(gla_src_p, gate_w, gate_b, gla_norm[0], None, heads=g_heads, batch=bp,
                           n_chunks=tp // c_p, c=c_p, t_valid=c_p, lr_off=lr_off, out_rows=n_all)
    pa_s, ps_s = sample_view(proj_a), sample_view(proj_s)
    gla_src_s = {"q": (pa_s, kb, 0), "k": (pa_s, kb, 1), "v": (pa_s, val_w, o_v // val_w),
                 "r": (pa_s, val_w, o_r // val_w), "lr": (ps_s, V7X_LANES, 0)}
    o_s, gla_s = _gla_call(gla_src_s, gate_w, gate_b, gla_norm[0], state_gla[0], heads=g_heads, batch=bs,
                           n_chunks=1, c=t_pad, t_valid=ts, lr_off=lr_off, out_rows=None)
    o_all = lax.dynamic_update_slice(o_p, o_s[:, :ts].reshape(n_s, val_w), (n_p, 0))

    bc_w = conv_dim - inner
    ssd_src_p = {"x": (proj_b, inner, 1), "bc": (proj_b, bc_w, (2 * inner) // bc_w),
                 "z": (proj_b, inner, 0), "dt": (proj_s, V7X_LANES, 0)}
    cs_p = math.gcd(tp, SSD_CHUNK)
    y_p, ssd_p, xbc_p = _ssd_call(
        ssd_src_p, ssd_conv_w[0], ssd_conv_b[0], ssd_dt_bias[0], ssd_a_log[0], ssd_d[0],
        ssd_norm[0], None, None, heads=s_heads, groups=s_groups, n=n_state, batch=bp,
        n_chunks=tp // cs_p, c=cs_p, t_valid=cs_p, dt_off=dt_off, out_rows=n_all)
    pb_s = sample_view(proj_b)
    ssd_src_s = {"x": (pb_s, inner, 1), "bc": (pb_s, bc_w, (2 * inner) // bc_w),
                 "z": (pb_s, inner, 0), "dt": (ps_s, V7X_LANES, 0)}
    y_s, ssd_s, xbc_s = _ssd_call(
        ssd_src_s, ssd_conv_w[0], ssd_conv_b[0], ssd_dt_bias[0], ssd_a_log[0], ssd_d[0],
        ssd_norm[0], state_conv[0], state_ssd[0], heads=s_heads, groups=s_groups,
        n=n_state, batch=bs, n_chunks=1, c=t_pad, t_valid=ts, dt_off=dt_off, out_rows=None)
    y_all = lax.dynamic_update_slice(y_p, y_s[:, :ts].reshape(n_s, inner), (n_p, 0))

    merged = _merge(o_all, y_all, w_proj_gla[0], w_proj_ssd[0], proj_c)
    x1 = _matmul(merged, w_out[0], res=x_all)

    h2 = _rmsnorm(x1, norm_cross[0], BF16)
    q_all = _matmul(h2, w_cq[0])
    mem = _rmsnorm(mem_prompt.reshape(-1, d), norm_mem[0], BF16)
    mk = _matmul(mem, w_ck[0])
    mv = _matmul(mem, w_cv[0])
    m_len = mem_prompt.shape[1]
    a_p = _attention(q_all[:n_p].reshape(bp, tp, mem_w), mk.reshape(bp, m_len, mem_w),
                     mv.reshape(bp, m_len, mem_w), heads=mem_heads)
    q_s = _pad_seq(q_all[n_p:].reshape(bs, ts, mem_w), t_pad)
    a_s = _attention(q_s, cache_mem_k[0].reshape(bs, -1, mem_w), cache_mem_v[0].reshape(bs, -1, mem_w),
                     heads=mem_heads)
    a_all = jnp.concatenate([a_p.reshape(n_p, mem_w), a_s[:, :ts].reshape(n_s, mem_w)], axis=0)
    x2 = _matmul(a_all, w_co[0], res=x1)

    tm_e = 512
    h3, top_pad, gate_pad = _router(x2, norm_ffn[0], router_w[0], router_b[0])
    dest, src, blk_e, n_used, nxt_e, cnt = _moe_layout(top_pad[:, :TOP_K], n_exp, tm_e)
    rows = _gather_rows(h3, src, n_used * tm_e, d)
    ys = _expert_ffn(rows, blk_e, n_used, nxt_e, cnt, moe_w1[0], moe_b1[0], moe_w2[0], moe_b2[0], tm=tm_e)
    y_fin = _combine(ys, dest, gate_pad, x2, norm_final)

    y_prompt = y_fin[:n_p].reshape(bp, tp, d)
    y_sample = y_fin[n_p:].reshape(bs, ts, d)
    return (y_prompt, y_sample, gla_p[None], ssd_p[None], xbc_p[None],
            mk.reshape(1, bp, m_len, mem_heads, mem_hd), mv.reshape(1, bp, m_len, mem_heads, mem_hd),
            gla_s[None], ssd_s[None], xbc_s[None])
```

```python
import functools
import math

import jax
import jax.numpy as jnp
from jax import lax
from jax.experimental import pallas as pl
from jax.experimental.pallas import tpu as pltpu

F32 = jnp.float32
BF16 = jnp.bfloat16

RMS_EPS = 1e-6
GLA_GATE_TAU = 16.0
GLA_CHUNK = 64
SSD_CHUNK = 64
TOP_K = 4
SWIGLU_LIMIT = 7.0
SWIGLU_ALPHA = 1.702

V7X_LANES = 128
V7X_SUBLANES = 8
V7X_VMEM_BYTES = 64 * 1024 * 1024
VMEM_LIMIT = V7X_VMEM_BYTES - 8 * 1024 * 1024

NT_DIMS = (((1,), (1,)), ((), ()))
TN_DIMS = (((0,), (0,)), ((), ()))


def _cparams(*sem):
    return pltpu.CompilerParams(dimension_semantics=sem, vmem_limit_bytes=VMEM_LIMIT)


def _tile(n, pref, quantum):
    t = (min(pref, n) // quantum) * quantum
    while t >= quantum:
        if n % t == 0:
            return t
        t -= quantum
    return n


def _dot(a, b):
    return jnp.dot(a.astype(BF16), b.astype(BF16), preferred_element_type=F32)


def _dot_nt(a, b):
    return lax.dot_general(a.astype(BF16), b.astype(BF16), NT_DIMS, preferred_element_type=F32)


def _dot_tn(a, b):
    return lax.dot_general(a.astype(BF16), b.astype(BF16), TN_DIMS, preferred_element_type=F32)


def _split3(x):
    hi = x.astype(BF16)
    r1 = x - hi.astype(F32)
    mid = r1.astype(BF16)
    lo = (r1 - mid.astype(F32)).astype(BF16)
    return hi, mid, lo


def _cumsum_rows(x, tri_bf):
    hi, mid, lo = _split3(x)
    n = x.shape[1]
    cat = jnp.concatenate([hi, mid, lo], axis=1)
    s = jnp.dot(tri_bf, cat, preferred_element_type=F32)
    return s[:, :n] + s[:, n:2 * n] + s[:, 2 * n:]


def _softplus(x):
    return jnp.maximum(x, 0.0) + jnp.log1p(jnp.exp(-jnp.abs(x)))


def _silu(x):
    return x * jax.nn.sigmoid(x)


def _tri_masks(c):
    row = lax.broadcasted_iota(jnp.int32, (c, c), 0)
    col = lax.broadcasted_iota(jnp.int32, (c, c), 1)
    mask = row >= col
    return mask, jnp.where(mask, 1.0, 0.0).astype(BF16)


def _rms_kernel(x_ref, g_ref, o_ref):
    x = x_ref[...]
    ms = jnp.mean(x * x, axis=-1, keepdims=True)
    o_ref[...] = (x * lax.rsqrt(ms + RMS_EPS) * g_ref[...]).astype(o_ref.dtype)


def _rmsnorm(x, g, out_dtype):
    n, d = x.shape
    tm = _tile(n, 256, V7X_SUBLANES)
    return pl.pallas_call(
        _rms_kernel,
        grid=(n // tm,),
        in_specs=[pl.BlockSpec((tm, d), lambda i: (i, 0)),
                  pl.BlockSpec((1, d), lambda i: (0, 0))],
        out_specs=pl.BlockSpec((tm, d), lambda i: (i, 0)),
        out_shape=jax.ShapeDtypeStruct((n, d), out_dtype),
        compiler_params=_cparams("parallel"),
        name="rmsnorm",
    )(x, g.reshape(1, d))


def _mm_kernel(*refs, has_res):
    if has_res:
        x_ref, w_ref, r_ref, o_ref, wb_ref = refs
    else:
        x_ref, w_ref, o_ref, wb_ref = refs
        r_ref = None

    @pl.when(pl.program_id(1) == 0)
    def _():
        wb_ref[...] = w_ref[...].astype(BF16)

    acc = jnp.dot(x_ref[...], wb_ref[...], preferred_element_type=F32)
    if r_ref is not None:
        acc = acc + r_ref[...]
    o_ref[...] = acc.astype(o_ref.dtype)


def _matmul(x, w, *, col0=0, ncols=None, res=None, out_dtype=F32, tm=512, tn=512):
    n, k = x.shape
    ncols = w.shape[1] - col0 if ncols is None else ncols
    tm = _tile(n, tm, V7X_SUBLANES)
    tn = _tile(math.gcd(ncols, col0) if col0 else ncols, tn, V7X_LANES)
    assert ncols % tn == 0 and col0 % tn == 0
    jb = col0 // tn
    in_specs = [pl.BlockSpec((tm, k), lambda j, i: (i, 0)),
                pl.BlockSpec((k, tn), lambda j, i: (0, j + jb))]
    args = [x, w]
    if res is not None:
        in_specs.append(pl.BlockSpec((tm, tn), lambda j, i: (i, j)))
        args.append(res)
    return pl.pallas_call(
        functools.partial(_mm_kernel, has_res=res is not None),
        grid=(ncols // tn, n // tm),
        in_specs=in_specs,
        out_specs=pl.BlockSpec((tm, tn), lambda j, i: (i, j)),
        out_shape=jax.ShapeDtypeStruct((n, ncols), out_dtype),
        scratch_shapes=[pltpu.VMEM((k, tn), BF16)],
        compiler_params=_cparams("arbitrary", "arbitrary"),
        name="matmul",
    )(*args)


def _mmt_kernel(x_ref, wt_hbm, o_ref, wland, wb_ref, sem, *, row0, tn):
    j = pl.program_id(0)

    def tile_copy(jj):
        rows = pl.ds(pl.multiple_of(row0 + jj * tn, V7X_SUBLANES), tn)
        return pltpu.make_async_copy(wt_hbm.at[rows, :], wland, sem)

    @pl.when(pl.program_id(1) == 0)
    def _():
        @pl.when(j == 0)
        def _():
            tile_copy(0).start()

        tile_copy(j).wait()
        wb_ref[...] = wland[...].astype(BF16)

        @pl.when(j + 1 < pl.num_programs(0))
        def _():
            tile_copy(j + 1).start()

    o_ref[...] = lax.dot_general(x_ref[...], wb_ref[...], NT_DIMS, preferred_element_type=F32)


def _matmul_t(x, wt, *, row0, nrows, tm=512, tn=1024):
    n, k = x.shape
    tm = _tile(n, tm, V7X_SUBLANES)
    tn = _tile(nrows, tn, V7X_LANES)
    assert row0 % V7X_SUBLANES == 0 and nrows % tn == 0
    return pl.pallas_call(
        functools.partial(_mmt_kernel, row0=row0, tn=tn),
        grid=(nrows // tn, n // tm),
        in_specs=[pl.BlockSpec((tm, k), lambda j, i: (i, 0)),
                  pl.BlockSpec(memory_space=pl.ANY)],
        out_specs=pl.BlockSpec((tm, tn), lambda j, i: (i, j)),
        out_shape=jax.ShapeDtypeStruct((n, nrows), F32),
        scratch_shapes=[pltpu.VMEM((tn, k), F32), pltpu.VMEM((tn, k), BF16),
                        pltpu.SemaphoreType.DMA(())],
        compiler_params=_cparams("arbitrary", "arbitrary"),
        name="matmul_t",
    )(x, wt)


def _merge_kernel(o_ref, y_ref, wg_hbm, ws_hbm, ga_ref, gb_ref, out_ref, wland, wgb_ref, wsb_ref, sem, *, tn):
    j = pl.program_id(0)

    def tile_copies(jj):
        cols = pl.ds(pl.multiple_of(jj * tn, tn), tn)
        return [pltpu.make_async_copy(w.at[:, cols], wland.at[s], sem.at[s])
                for s, w in enumerate((wg_hbm, ws_hbm))]

    @pl.when(pl.program_id(1) == 0)
    def _():
        @pl.when(j == 0)
        def _():
            for cp in tile_copies(0):
                cp.start()

        for cp in tile_copies(j):
            cp.wait()
        wgb_ref[...] = wland[0].astype(BF16)
        wsb_ref[...] = wland[1].astype(BF16)

        @pl.when(j + 1 < pl.num_programs(0))
        def _():
            for cp in tile_copies(j + 1):
                cp.start()

    pg = jnp.dot(o_ref[...], wgb_ref[...], preferred_element_type=F32)
    ps = jnp.dot(y_ref[...], wsb_ref[...], preferred_element_type=F32)
    out = jax.nn.sigmoid(ga_ref[...]) * pg + jax.nn.sigmoid(gb_ref[...]) * ps
    out_ref[...] = out.astype(out_ref.dtype)


def _merge(o_gla, y_ssd, w_pg, w_ps, gates, tm=256, tn=512):
    n, k = o_gla.shape
    d = w_pg.shape[1]
    tm = _tile(n, tm, V7X_SUBLANES)
    tn = _tile(d, tn, V7X_LANES)
    nj = d // tn
    return pl.pallas_call(
        functools.partial(_merge_kernel, tn=tn),
        grid=(nj, n // tm),
        in_specs=[pl.BlockSpec((tm, k), lambda j, i: (i, 0)),
                  pl.BlockSpec((tm, k), lambda j, i: (i, 0)),
                  pl.BlockSpec(memory_space=pl.ANY),
                  pl.BlockSpec(memory_space=pl.ANY),
                  pl.BlockSpec((tm, tn), lambda j, i: (i, j)),
                  pl.BlockSpec((tm, tn), lambda j, i: (i, j + nj))],
        out_specs=pl.BlockSpec((tm, tn), lambda j, i: (i, j)),
        out_shape=jax.ShapeDtypeStruct((n, d), BF16),
        scratch_shapes=[pltpu.VMEM((2, k, tn), F32), pltpu.VMEM((k, tn), BF16), pltpu.VMEM((k, tn), BF16),
                        pltpu.SemaphoreType.DMA((2,))],
        compiler_params=_cparams("arbitrary", "arbitrary"),
        name="merge",
    )(o_gla, y_ssd, w_pg, w_ps, gates, gates)


def _gla_kernel(*refs, heads, dk, dv, c, t_valid, lr_off, rank, has_s0):
    if has_s0:
        q_ref, k_ref, v_ref, r_ref, lr_ref, gw_ref, gb_ref, gn_ref, s0_ref, o_ref, so_ref, s_scr = refs
    else:
        q_ref, k_ref, v_ref, r_ref, lr_ref, gw_ref, gb_ref, gn_ref, o_ref, so_ref, s_scr = refs
        s0_ref = None
    ci = pl.program_id(1)

    @pl.when(ci == 0)
    def _():
        if s0_ref is None:
            s_scr[...] = jnp.zeros_like(s_scr)
        else:
            s_scr[...] = s0_ref[...]

    mask, tri_bf = _tri_masks(c)
    kw = heads * dk
    scale = dk ** -0.5
    gn = gn_ref[...]
    lr = lr_ref[:, lr_off:lr_off + rank]
    zg = jnp.dot(lr, gw_ref[...], preferred_element_type=F32) + gb_ref[...]
    la = (jnp.minimum(zg, 0.0) - jnp.log1p(jnp.exp(-jnp.abs(zg)))) * (1.0 / GLA_GATE_TAU)
    if t_valid < c:
        la = jnp.where(lax.broadcasted_iota(jnp.int32, (c, kw), 0) < t_valid, la, 0.0)
    b = _cumsum_rows(la, tri_bf)
    k_all = k_ref[...]
    qe = ((q_ref[...] * scale) * jnp.exp(b)).astype(BF16)
    ke = (k_all * jnp.exp(-b)).astype(BF16)
    b_last = b[c - 1:c, :]
    kd = (k_all * jnp.exp(b_last - b)).astype(BF16)
    e_hi, e_mid, e_lo = (t.astype(F32) for t in _split3(jnp.exp(b_last)))
    row8 = lax.broadcasted_iota(jnp.int32, (V7X_SUBLANES, kw), 0)
    e3 = jnp.where(row8 == 0, e_hi, jnp.where(row8 == 1, e_mid, jnp.where(row8 == 2, e_lo, 0.0)))
    ones8 = jnp.ones((V7X_SUBLANES, dv), F32)
    atts = []
    for h in range(heads):
        ks = slice(h * dk, (h + 1) * dk)
        att = lax.dot_general(qe[:, ks], ke[:, ks], NT_DIMS, preferred_element_type=F32)
        atts.append(jnp.where(mask, att, 0.0).astype(BF16))
    for h in range(heads):
        ks = slice(h * dk, (h + 1) * dk)
        vs = slice(h * dv, (h + 1) * dv)
        v = v_ref[:, vs].astype(BF16)
        s_old = s_scr[h]
        o = (jnp.dot(atts[h], v, preferred_element_type=F32)
             + jnp.dot(qe[:, ks], s_old.astype(BF16), preferred_element_type=F32))
        e_full = lax.dot_general(e3[:, ks], ones8, TN_DIMS, preferred_element_type=F32)
        s_scr[h] = e_full * s_old + lax.dot_general(kd[:, ks], v, TN_DIMS, preferred_element_type=F32)
        ms = jnp.mean(o * o, axis=-1, keepdims=True)
        og = o * lax.rsqrt(ms + RMS_EPS) * gn * _silu(r_ref[:, vs])
        o_ref[:, vs] = og.astype(o_ref.dtype)

    @pl.when(ci == pl.num_programs(1) - 1)
    def _():
        so_ref[...] = s_scr[...]


def _gla_call(srcs, gate_w, gate_b, gnorm, s0, *, heads, batch, n_chunks, c, t_valid, lr_off, out_rows):
    rank, kw = gate_w.shape
    dk = kw // heads
    dv = gnorm.shape[-1]
    vw = heads * dv

    def spec(name):
        arr, width, cb = srcs[name]
        if arr.ndim == 3:
            return pl.BlockSpec((None, c, width), lambda b, ci: (b, ci, cb))
        return pl.BlockSpec((c, width), lambda b, ci: (b * n_chunks + ci, cb))

    names = ("q", "k", "v", "r", "lr")
    in_specs = [spec(nm) for nm in names] + [
        pl.BlockSpec((rank, kw), lambda b, ci: (0, 0)),
        pl.BlockSpec((1, kw), lambda b, ci: (0, 0)),
        pl.BlockSpec((1, dv), lambda b, ci: (0, 0)),
    ]
    args = [srcs[nm][0] for nm in names] + [gate_w, gate_b.reshape(1, kw), gnorm.reshape(1, dv)]
    if s0 is not None:
        in_specs.append(pl.BlockSpec((None, heads, dk, dv), lambda b, ci: (b, 0, 0, 0)))
        args.append(s0)
    three_d = srcs["q"][0].ndim == 3
    if three_d:
        o_spec = pl.BlockSpec((None, c, vw), lambda b, ci: (b, ci, 0))
        o_shape = jax.ShapeDtypeStruct((batch, n_chunks * c, vw), BF16)
    else:
        o_spec = pl.BlockSpec((c, vw), lambda b, ci: (b * n_chunks + ci, 0))
        o_shape = jax.ShapeDtypeStruct((out_rows, vw), BF16)
    return pl.pallas_call(
        functools.partial(_gla_kernel, heads=heads, dk=dk, dv=dv, c=c, t_valid=t_valid,
                          lr_off=lr_off, rank=rank, has_s0=s0 is not None),
        grid=(batch, n_chunks),
        in_specs=in_specs,
        out_specs=[o_spec, pl.BlockSpec((None, heads, dk, dv), lambda b, ci: (b, 0, 0, 0))],
        out_shape=[o_shape, jax.ShapeDtypeStruct((batch, heads, dk, dv), F32)],
        scratch_shapes=[pltpu.VMEM((heads, dk, dv), F32)],
        compiler_params=_cparams("parallel", "arbitrary"),
        name="gla",
    )(*args)


def _ssd_kernel(*refs, heads, groups, p, n, c, t_valid, dt_off, conv_w, has_state):
    if has_state:
        (x_ref, bc_ref, z_ref, dt_ref, cw_ref, cb_ref, dtb_ref, alog_ref, dexp_ref, nrm_ref,
         cs_ref, h0_ref, y_ref, ho_ref, cso_ref, h_scr, xf_scr, bcf_scr, y_scr) = refs
    else:
        (x_ref, bc_ref, z_ref, dt_ref, cw_ref, cb_ref, dtb_ref, alog_ref, dexp_ref, nrm_ref,
         y_ref, ho_ref, cso_ref, h_scr, xf_scr, bcf_scr, y_scr) = refs
        cs_ref = h0_ref = None
    ci = pl.program_id(1)
    inner = heads * p
    gn = groups * n
    r_per_g = heads // groups
    pad = V7X_SUBLANES
    hist = conv_w - 1

    @pl.when(ci == 0)
    def _():
        if has_state:
            h_scr[...] = h0_ref[...]
            xf_scr[0:pad, :] = jnp.zeros((pad, inner), F32)
            bcf_scr[0:pad, :] = jnp.zeros((pad, 2 * gn), F32)
            xf_scr[pad - hist:pad, :] = cs_ref[:, 0:inner]
            bcf_scr[pad - hist:pad, :] = cs_ref[:, inner:inner + 2 * gn]
        else:
            h_scr[...] = jnp.zeros_like(h_scr)
            xf_scr[0:pad, :] = jnp.zeros((pad, inner), F32)
            bcf_scr[0:pad, :] = jnp.zeros((pad, 2 * gn), F32)

    xf_scr[pad:pad + c, :] = x_ref[...]
    bcf_scr[pad:pad + c, :] = bc_ref[...]
    xc = cb_ref[:, 0:inner] + xf_scr[pad - hist:pad - hist + c, :] * cw_ref[0:1, 0:inner]
    bcc = cb_ref[:, inner:] + bcf_scr[pad - hist:pad - hist + c, :] * cw_ref[0:1, inner:]
    for j in range(1, conv_w):
        xc = xc + xf_scr[pad - hist + j:pad - hist + j + c, :] * cw_ref[j:j + 1, 0:inner]
        bcc = bcc + bcf_scr[pad - hist + j:pad - hist + j + c, :] * cw_ref[j:j + 1, inner:]
    xs = _silu(xc)
    bcs = _silu(bcc)
    xf_scr[0:pad, :] = x_ref[c - pad:c, :]
    bcf_scr[0:pad, :] = bc_ref[c - pad:c, :]

    dt = _softplus(dt_ref[:, dt_off:dt_off + heads] + dtb_ref[...])
    if t_valid < c:
        dt = jnp.where(lax.broadcasted_iota(jnp.int32, (c, heads), 0) < t_valid, dt, 0.0)
    a_neg = -jnp.exp(alog_ref[...])
    mask, tri_bf = _tri_masks(c)
    cum = _cumsum_rows(dt * a_neg, tri_bf)
    sq = V7X_LANES
    both = jnp.concatenate([cum, dt], axis=1)
    if both.shape[1] < sq:
        both = jnp.concatenate([both, jnp.zeros((c, sq - both.shape[1]), F32)], axis=1)
    if c < sq:
        both = jnp.concatenate([both, jnp.zeros((sq - c, sq), F32)], axis=0)
    both_t = both.T
    e_cum = jnp.exp(cum)
    last = cum[c - 1:c, :]
    wk = jnp.exp(last - cum) * dt
    e_last = jnp.exp(last)

    paired = c == p and 2 * p == V7X_LANES and r_per_g % 2 == 0
    if paired:
        lane2 = lax.broadcasted_iota(jnp.int32, (c, 2 * p), 1)
        lo = lane2 < p
        mask2 = lax.broadcasted_iota(jnp.int32, (c, 2 * p), 0) >= jnp.where(lo, lane2, lane2 - p)
    for g in range(groups):
        bg = bcs[:, g * n:(g + 1) * n]
        cg = bcs[:, gn + g * n:gn + (g + 1) * n]
        cbm = _dot_nt(cg, bg)
        hg = h_scr[g]
        ych = _dot_nt(cg, hg)
        xw_parts = []
        if paired:
            cb2 = jnp.concatenate([cbm, cbm], axis=1)
            for pr in range(r_per_g // 2):
                ha = g * r_per_g + 2 * pr
                hb = ha + 1
                lanes = slice(ha * p, (ha + 2) * p)

                def both(arr, ha=ha, hb=hb):
                    return jnp.where(lo, arr[:, ha:ha + 1], arr[:, hb:hb + 1])

                def rows(r0, ha=ha, hb=hb):
                    return jnp.concatenate([both_t[r0 + ha:r0 + ha + 1, 0:c],
                                            both_t[r0 + hb:r0 + hb + 1, 0:c]], axis=1)

                decay = jnp.exp(jnp.where(mask2, both(cum) - rows(0), -jnp.inf))
                w = cb2 * decay * rows(heads)
                xp = xs[:, lanes]
                x2 = jnp.concatenate([jnp.where(lo, xp, 0.0), jnp.where(lo, 0.0, xp)], axis=0)
                y_scr[:, lanes] = _dot(w, x2) + ych[:, 2 * pr * p:(2 * pr + 2) * p] * both(e_cum)
                xw_parts.append(xp * both(wk))
        for r in range(0 if paired else r_per_g):
            hd = g * r_per_g + r
            cum_col = cum[:, hd:hd + 1]
            cum_row = both_t[hd:hd + 1, 0:c]
            dt_row = both_t[heads + hd:heads + hd + 1, 0:c]
            decay = jnp.exp(jnp.where(mask, cum_col - cum_row, -jnp.inf))
            w = cbm * decay * dt_row
            xh = xs[:, hd * p:(hd + 1) * p]
            yh = _dot(w, xh) + ych[:, r * p:(r + 1) * p] * e_cum[:, hd:hd + 1]
            y_scr[:, hd * p:(hd + 1) * p] = yh
            xw_parts.append(xh * wk[:, hd:hd + 1])
        xw = jnp.concatenate(xw_parts, axis=1)
        upd = _dot_tn(xw, bg)
        for r in range(r_per_g):
            hd = g * r_per_g + r
            rs = slice(r * p, (r + 1) * p)
            h_scr[g, rs, :] = e_last[:, hd:hd + 1] * hg[rs, :] + upd[rs, :]

    y = (y_scr[...] + dexp_ref[...] * xs) * _silu(z_ref[...])
    gw = inner // groups
    for g in range(groups):
        gs = slice(g * gw, (g + 1) * gw)
        yg = y[:, gs]
        ms = jnp.mean(yg * yg, axis=-1, keepdims=True)
        y_ref[:, gs] = (yg * lax.rsqrt(ms + RMS_EPS) * nrm_ref[:, gs]).astype(y_ref.dtype)

    @pl.when(ci == pl.num_programs(1) - 1)
    def _():
        ho_ref[...] = h_scr[...]
        cso_ref[:, 0:inner] = x_ref[t_valid - hist:t_valid, :]
        cso_ref[:, inner:] = bc_ref[t_valid - hist:t_valid, :]


def _ssd_call(srcs, conv_w, conv_b, dt_bias, a_log, d_skip, ssd_norm, conv_state, h0, *,
              heads, groups, n, batch, n_chunks, c, t_valid, dt_off, out_rows):
    kw, conv_dim = conv_w.shape
    inner = ssd_norm.shape[-1]
    p = inner // heads
    gn = groups * n
    r_per_g = heads // groups

    def spec(name):
        arr, width, cb = srcs[name]
        if arr.ndim == 3:
            return pl.BlockSpec((None, c, width), lambda b, ci: (b, ci, cb))
        return pl.BlockSpec((c, width), lambda b, ci: (b * n_chunks + ci, cb))

    def full(shape):
        return pl.BlockSpec(shape, lambda b, ci: (0,) * len(shape))

    names = ("x", "bc", "z", "dt")
    in_specs = [spec(nm) for nm in names] + [
        full((kw, conv_dim)), full((1, conv_dim)), full((1, heads)), full((1, heads)),
        full((1, inner)), full((1, inner))]
    args = [srcs[nm][0] for nm in names] + [
        conv_w, conv_b.reshape(1, conv_dim), dt_bias.reshape(1, heads), a_log.reshape(1, heads),
        jnp.repeat(d_skip, p).reshape(1, inner), ssd_norm.reshape(1, inner)]
    has_state = h0 is not None
    if has_state:
        in_specs += [pl.BlockSpec((None, kw - 1, conv_dim), lambda b, ci: (b, 0, 0)),
                     pl.BlockSpec((None, groups, r_per_g * p, n), lambda b, ci: (b, 0, 0, 0))]
        args += [conv_state, h0.reshape(batch, groups, r_per_g * p, n)]
    three_d = srcs["x"][0].ndim == 3
    if three_d:
        y_spec = pl.BlockSpec((None, c, inner), lambda b, ci: (b, ci, 0))
        y_shape = jax.ShapeDtypeStruct((batch, n_chunks * c, inner), BF16)
    else:
        y_spec = pl.BlockSpec((c, inner), lambda b, ci: (b * n_chunks + ci, 0))
        y_shape = jax.ShapeDtypeStruct((out_rows, inner), BF16)
    assert t_valid >= kw - 1, "the new conv state must lie inside the last chunk"
    y, h_new, conv_new = pl.pallas_call(
        functools.partial(_ssd_kernel, heads=heads, groups=groups, p=p, n=n, c=c, t_valid=t_valid,
                          dt_off=dt_off, conv_w=kw, has_state=has_state),
        grid=(batch, n_chunks),
        in_specs=in_specs,
        out_specs=[y_spec, pl.BlockSpec((None, groups, r_per_g * p, n), lambda b, ci: (b, 0, 0, 0)),
                   pl.BlockSpec((None, kw - 1, conv_dim), lambda b, ci: (b, 0, 0))],
        out_shape=[y_shape, jax.ShapeDtypeStruct((batch, groups, r_per_g * p, n), F32),
                   jax.ShapeDtypeStruct((batch, kw - 1, conv_dim), F32)],
        scratch_shapes=[pltpu.VMEM((groups, r_per_g * p, n), F32),
                        pltpu.VMEM((V7X_SUBLANES + c, inner), F32),
                        pltpu.VMEM((V7X_SUBLANES + c, 2 * gn), F32),
                        pltpu.VMEM((c, inner), F32)],
        compiler_params=_cparams("parallel", "arbitrary"),
        name="ssd",
    )(*args)
    return y, h_new.reshape(batch, heads, p, n), conv_new


def _attn_kernel(q_ref, k_ref, v_ref, o_ref, *, heads, hd):
    scale = hd ** -0.5
    for h in range(heads):
        hs = slice(h * hd, (h + 1) * hd)
        s = _dot_nt(q_ref[:, hs], k_ref[:, hs]) * scale
        m = jnp.max(s, axis=-1, keepdims=True)
        e = jnp.exp(s - m)
        prob = e / jnp.sum(e, axis=-1, keepdims=True)
        o_ref[:, hs] = _dot(prob, v_ref[:, hs]).astype(o_ref.dtype)


def _attention(q, mem_k, mem_v, *, heads, tq=512):
    batch, t, w = q.shape
    m = mem_k.shape[1]
    tq = _tile(t, tq, V7X_SUBLANES)
    return pl.pallas_call(
        functools.partial(_attn_kernel, heads=heads, hd=w // heads),
        grid=(batch, t // tq),
        in_specs=[pl.BlockSpec((None, tq, w), lambda b, i: (b, i, 0)),
                  pl.BlockSpec((None, m, w), lambda b, i: (b, 0, 0)),
                  pl.BlockSpec((None, m, w), lambda b, i: (b, 0, 0))],
        out_specs=pl.BlockSpec((None, tq, w), lambda b, i: (b, i, 0)),
        out_shape=jax.ShapeDtypeStruct((batch, t, w), BF16),
        compiler_params=_cparams("parallel", "arbitrary"),
        name="cross_attention",
    )(q, mem_k, mem_v)


def _router_kernel(x_ref, g_ref, rw_ref, rb_ref, h_ref, idx_ref, gate_ref, *, n_exp, top_k):
    x = x_ref[...]
    ms = jnp.mean(x * x, axis=-1, keepdims=True)
    h = x * lax.rsqrt(ms + RMS_EPS) * g_ref[...]
    tm, d = x.shape
    nseg = d // V7X_LANES
    pitch = _token_pitch(d)
    for s in range(nseg):
        h_ref[pl.ds(s, tm, stride=pitch), :] = h[:, s * V7X_LANES:(s + 1) * V7X_LANES]
    h_hi, h_mid, h_lo = _split3(h)
    w_hi, w_mid, w_lo = _split3(rw_ref[...])

    def d(a, b):
        return jnp.dot(a, b, preferred_element_type=F32)

    logits = (d(h_hi, w_hi) + (d(h_hi, w_mid) + d(h_mid, w_hi))
              + (d(h_hi, w_lo) + d(h_mid, w_mid) + d(h_lo, w_hi))) + rb_ref[...]
    lanes = lax.broadcasted_iota(jnp.int32, logits.shape, 1).astype(F32)
    work = logits
    vals, idxs = [], []
    for _ in range(top_k):
        mx = jnp.max(work, axis=-1, keepdims=True)
        ix = jnp.min(jnp.where(work == mx, lanes, float(n_exp)), axis=-1, keepdims=True)
        vals.append(mx)
        idxs.append(ix)
        work = jnp.where(lanes == ix, -jnp.inf, work)
    es = [jnp.exp(v - vals[0]) for v in vals]
    tot = es[0]
    for e in es[1:]:
        tot = tot + e
    out_lanes = lax.broadcasted_iota(jnp.int32, idx_ref.shape, 1)
    idx_out = jnp.zeros(idx_ref.shape, jnp.int32)
    gate_out = jnp.zeros(gate_ref.shape, F32)
    for j in range(top_k):
        idx_out = jnp.where(out_lanes == j, idxs[j].astype(jnp.int32), idx_out)
        gate_out = jnp.where(out_lanes == j, es[j] / tot, gate_out)
    idx_ref[...] = idx_out
    gate_ref[...] = gate_out


def _router(x, g, router_w, router_b):
    n, d = x.shape
    n_exp = router_w.shape[1]
    tm = _tile(n, 256, V7X_SUBLANES)
    nseg = _token_pitch(d)
    return pl.pallas_call(
        functools.partial(_router_kernel, n_exp=n_exp, top_k=TOP_K),
        grid=(n // tm,),
        in_specs=[pl.BlockSpec((tm, d), lambda i: (i, 0)),
                  pl.BlockSpec((1, d), lambda i: (0, 0)),
                  pl.BlockSpec((d, n_exp), lambda i: (0, 0)),
                  pl.BlockSpec((1, n_exp), lambda i: (0, 0))],
        out_specs=[pl.BlockSpec((tm * nseg, V7X_LANES), lambda i: (i, 0)),
                   pl.BlockSpec((tm, V7X_LANES), lambda i: (i, 0)),
                   pl.BlockSpec((tm, V7X_LANES), lambda i: (i, 0))],
        out_shape=[jax.ShapeDtypeStruct((n * nseg, V7X_LANES), F32),
                   jax.ShapeDtypeStruct((n, V7X_LANES), jnp.int32),
                   jax.ShapeDtypeStruct((n, V7X_LANES), F32)],
        compiler_params=_cparams("parallel"),
        name="router",
    )(x, g.reshape(1, d), router_w, router_b.reshape(1, n_exp))


def _row_copy(src_hbm, row, dst, slot, sem):
    return pltpu.make_async_copy(src_hbm.at[pl.ds(row, 1), :], dst.at[pl.ds(slot, 1), :], sem)


def _token_pitch(d):
    return d // V7X_LANES + 1


def _token_copy(h_hbm, tok, buf, slot, sem, nseg, pitch):
    src = h_hbm.at[pl.ds(tok * pitch, nseg), :]
    dst = buf.at[pl.ds(slot * pitch, nseg), :]
    return pltpu.make_async_copy(src, dst, sem)


GATHER_UNROLL = 8


def _gather_kernel(nrows_ref, idx_ref, h_hbm, o_ref, buf, sem, *, tg, nseg, pitch):
    @pl.when(pl.program_id(0) * tg < nrows_ref[0])
    def _():
        def start(r8, carry):
            for u in range(GATHER_UNROLL):
                r = r8 * GATHER_UNROLL + u
                _token_copy(h_hbm, idx_ref[0, 0, r], buf, r, sem, nseg, pitch).start(priority=u % 2)
            return carry

        lax.fori_loop(0, tg // GATHER_UNROLL, start, 0)

        def wait(r, carry):
            _token_copy(h_hbm, 0, buf, r, sem, nseg, pitch).wait()
            return carry

        lax.fori_loop(0, tg, wait, 0, unroll=GATHER_UNROLL)
        for s in range(nseg):
            seg = buf[pl.ds(s, tg, stride=pitch), :]
            o_ref[:, s * V7X_LANES:(s + 1) * V7X_LANES] = seg.astype(o_ref.dtype)


def _gather_rows(h_tok, src, n_rows, d, tg=256):
    m = src.shape[0]
    nseg = d // V7X_LANES
    pitch = _token_pitch(d)
    return pl.pallas_call(
        functools.partial(_gather_kernel, tg=tg, nseg=nseg, pitch=pitch),
        grid_spec=pltpu.PrefetchScalarGridSpec(
            num_scalar_prefetch=1,
            grid=(m // tg,),
            in_specs=[pl.BlockSpec((1, 1, tg), lambda i, nr: (i, 0, 0), memory_space=pltpu.SMEM),
                      pl.BlockSpec(memory_space=pl.ANY)],
            out_specs=pl.BlockSpec((tg, d), lambda i, nr: (i, 0)),
            scratch_shapes=[pltpu.VMEM((tg * pitch, V7X_LANES), F32), pltpu.SemaphoreType.DMA(())]),
        out_shape=jax.ShapeDtypeStruct((m, d), BF16),
        compiler_params=_cparams("arbitrary"),
        name="moe_gather",
    )(n_rows, src.reshape(m // tg, 1, tg), h_tok)


MOE_SUB_ROWS = 128


def _expert_weights(be_ref, nxt_ref, cnt_ref, copies, cast):
    j, i = pl.program_id(0), pl.program_id(1)
    first = jnp.logical_and(cnt_ref[i] > 0,
                            jnp.logical_or(i == 0, be_ref[i] != be_ref[jnp.maximum(i - 1, 0)]))

    @pl.when(first)
    def _():
        @pl.when(jnp.logical_and(j == 0, i == 0))
        def _():
            for cp in copies(be_ref[0], 0):
                cp.start()

        for cp in copies(be_ref[i], j):
            cp.wait()
        cast()
        nxt = nxt_ref[i]

        @pl.when(nxt >= 0)
        def _():
            for cp in copies(nxt, j):
                cp.start()

        @pl.when(jnp.logical_and(nxt < 0, j + 1 < pl.num_programs(0)))
        def _():
            for cp in copies(be_ref[0], j + 1):
                cp.start()


def _ffn1_kernel(be_ref, nu_ref, nxt_ref, cnt_ref, x_ref, w1_hbm, bg_ref, bu_ref, o_ref,
                 wland, wgb_ref, wub_ref, sem, *, ff, tf):
    def copies(e, jj):
        return [pltpu.make_async_copy(w1_hbm.at[e, :, pl.ds(pl.multiple_of(half * ff + jj * tf, tf), tf)],
                                      wland.at[half], sem.at[half]) for half in range(2)]

    def cast():
        wgb_ref[...] = wland[0].astype(BF16)
        wub_ref[...] = wland[1].astype(BF16)

    _expert_weights(be_ref, nxt_ref, cnt_ref, copies, cast)
    i = pl.program_id(1)
    for s in range(x_ref.shape[0] // MOE_SUB_ROWS):
        @pl.when(s * MOE_SUB_ROWS < cnt_ref[i])
        def _():
            rs = slice(s * MOE_SUB_ROWS, (s + 1) * MOE_SUB_ROWS)
            x = x_ref[rs, :]
            g_ = jnp.dot(x, wgb_ref[...], preferred_element_type=F32) + bg_ref[...]
            u_ = jnp.dot(x, wub_ref[...], preferred_element_type=F32) + bu_ref[...]
            g_ = jnp.minimum(g_, SWIGLU_LIMIT)
            u_ = jnp.clip(u_, -SWIGLU_LIMIT, SWIGLU_LIMIT)
            act = (u_ + 1.0) * g_ * jax.nn.sigmoid(SWIGLU_ALPHA * g_)
            o_ref[rs, :] = act.astype(o_ref.dtype)


def _ffn2_kernel(be_ref, nu_ref, nxt_ref, cnt_ref, x_ref, w2_hbm, b_ref, o_ref, wland, wb_ref, sem, *, tn):
    def copies(e, jj):
        return [pltpu.make_async_copy(w2_hbm.at[e, :, pl.ds(pl.multiple_of(jj * tn, tn), tn)], wland, sem)]

    def cast():
        wb_ref[...] = wland[...].astype(BF16)

    _expert_weights(be_ref, nxt_ref, cnt_ref, copies, cast)
    i = pl.program_id(1)
    for s in range(x_ref.shape[0] // MOE_SUB_ROWS):
        @pl.when(s * MOE_SUB_ROWS < cnt_ref[i])
        def _():
            rs = slice(s * MOE_SUB_ROWS, (s + 1) * MOE_SUB_ROWS)
            o_ref[rs, :] = jnp.dot(x_ref[rs, :], wb_ref[...], preferred_element_type=F32) + b_ref[...]


def _expert_ffn(rows, blk_e, n_used, nxt_e, cnt, w1, b1, w2, b2, *, tm, tf=512, tn=1024):
    m_pad, d = rows.shape
    n_exp, _, ff2 = w1.shape
    ff = ff2 // 2
    n_blk = m_pad // tm
    tf = _tile(ff, tf, V7X_LANES)
    tn = _tile(d, tn, V7X_LANES)
    nf = ff // tf
    assert tm % MOE_SUB_ROWS == 0

    def row_blk(i, nu):
        return jnp.minimum(i, nu[0] - 1)

    act = pl.pallas_call(
        functools.partial(_ffn1_kernel, ff=ff, tf=tf),
        grid_spec=pltpu.PrefetchScalarGridSpec(
            num_scalar_prefetch=4,
            grid=(nf, n_blk),
            in_specs=[pl.BlockSpec((tm, d), lambda j, i, be, nu, nx, ct: (row_blk(i, nu), 0)),
                      pl.BlockSpec(memory_space=pl.ANY),
                      pl.BlockSpec((None, 1, tf), lambda j, i, be, nu, nx, ct: (be[i], 0, j)),
                      pl.BlockSpec((None, 1, tf), lambda j, i, be, nu, nx, ct: (be[i], 0, j + nf))],
            out_specs=pl.BlockSpec((tm, tf), lambda j, i, be, nu, nx, ct: (row_blk(i, nu), j)),
            scratch_shapes=[pltpu.VMEM((2, d, tf), F32), pltpu.VMEM((d, tf), BF16),
                            pltpu.VMEM((d, tf), BF16), pltpu.SemaphoreType.DMA((2,))]),
        out_shape=jax.ShapeDtypeStruct((m_pad, ff), BF16),
        compiler_params=_cparams("arbitrary", "arbitrary"),
        name="moe_ffn1",
    )(blk_e, n_used, nxt_e, cnt, rows, w1, b1.reshape(n_exp, 1, ff2), b1.reshape(n_exp, 1, ff2))

    return pl.pallas_call(
        functools.partial(_ffn2_kernel, tn=tn),
        grid_spec=pltpu.PrefetchScalarGridSpec(
            num_scalar_prefetch=4,
            grid=(d // tn, n_blk),
            in_specs=[pl.BlockSpec((tm, ff), lambda j, i, be, nu, nx, ct: (row_blk(i, nu), 0)),
                      pl.BlockSpec(memory_space=pl.ANY),
                      pl.BlockSpec((None, 1, tn), lambda j, i, be, nu, nx, ct: (be[i], 0, j))],
            out_specs=pl.BlockSpec((tm, tn), lambda j, i, be, nu, nx, ct: (row_blk(i, nu), j)),
            scratch_shapes=[pltpu.VMEM((ff, tn), F32), pltpu.VMEM((ff, tn), BF16),
                            pltpu.SemaphoreType.DMA(())]),
        out_shape=jax.ShapeDtypeStruct((m_pad, d), F32),
        compiler_params=_cparams("arbitrary", "arbitrary"),
        name="moe_ffn2",
    )(blk_e, n_used, nxt_e, cnt, act, w2, b2.reshape(n_exp, 1, d))


def _combine_kernel(dest_ref, ys_hbm, gate_ref, x_ref, g_ref, o_ref, buf, sem, *, tc, top_k):
    def start(r, carry):
        for j in range(top_k):
            _row_copy(ys_hbm, dest_ref[0, 0, r * top_k + j], buf.at[j], r, sem).start()
        return carry

    lax.fori_loop(0, tc, start, 0)

    def wait(r, carry):
        for j in range(top_k):
            _row_copy(ys_hbm, 0, buf.at[j], r, sem).wait()
        return carry

    lax.fori_loop(0, tc, wait, 0)
    gate = gate_ref[...]
    y = buf[0] * gate[:, 0:1]
    for j in range(1, top_k):
        y = y + buf[j] * gate[:, j:j + 1]
    x = x_ref[...] + y
    ms = jnp.mean(x * x, axis=-1, keepdims=True)
    o_ref[...] = x * lax.rsqrt(ms + RMS_EPS) * g_ref[...]


def _combine(ys, dest, gate, x, g, tc=128):
    n, d = x.shape
    tc = _tile(n, tc, V7X_SUBLANES)
    return pl.pallas_call(
        functools.partial(_combine_kernel, tc=tc, top_k=TOP_K),
        grid=(n // tc,),
        in_specs=[pl.BlockSpec((1, 1, tc * TOP_K), lambda i: (i, 0, 0), memory_space=pltpu.SMEM),
                  pl.BlockSpec(memory_space=pl.ANY),
                  pl.BlockSpec((tc, V7X_LANES), lambda i: (i, 0)),
                  pl.BlockSpec((tc, d), lambda i: (i, 0)),
                  pl.BlockSpec((1, d), lambda i: (0, 0))],
        out_specs=pl.BlockSpec((tc, d), lambda i: (i, 0)),
        out_shape=jax.ShapeDtypeStruct((n, d), F32),
        scratch_shapes=[pltpu.VMEM((TOP_K, tc, d), F32), pltpu.SemaphoreType.DMA(())],
        compiler_params=_cparams("arbitrary"),
        name="moe_combine",
    )(dest.reshape(n // tc, 1, tc * TOP_K), ys, gate, x, g.reshape(1, d))


def _moe_layout(top_i, n_exp, tm):
    n_tok = top_i.shape[0]
    m = n_tok * TOP_K
    e_flat = top_i.reshape(-1)
    onehot = (e_flat[:, None] == jnp.arange(n_exp, dtype=jnp.int32)[None, :]).astype(jnp.int32)
    rank = jnp.sum((jnp.cumsum(onehot, axis=0) - onehot) * onehot, axis=1)
    counts = jnp.sum(onehot, axis=0)
    pad_counts = (counts + tm - 1) // tm * tm
    pad_end = jnp.cumsum(pad_counts)
    pad_start = pad_end - pad_counts
    dest = pad_start[e_flat] + rank
    n_blk = (m + n_exp * (tm - 1)) // tm + 1
    tok = jnp.arange(m, dtype=jnp.int32) // TOP_K
    src = jnp.zeros((n_blk * tm,), jnp.int32).at[dest].set(tok)
    blk_start = jnp.arange(n_blk, dtype=jnp.int32) * tm
    blk_e = jnp.minimum(jnp.sum(pad_end[None, :] <= blk_start[:, None], axis=1), n_exp - 1).astype(jnp.int32)
    n_used = (pad_end[-1] // tm).astype(jnp.int32)
    blk = jnp.arange(n_blk, dtype=jnp.int32)
    used = blk < n_used
    cnt = jnp.where(used, jnp.clip(counts[blk_e] - (blk_start - pad_start[blk_e]), 0, tm), 0).astype(jnp.int32)
    starts = jnp.logical_and(used, jnp.concatenate([jnp.ones((1,), bool), blk_e[1:] != blk_e[:-1]]))
    start_idx = jnp.where(starts, blk, n_blk)
    nxt_idx = jnp.concatenate([lax.cummin(start_idx, reverse=True)[1:], jnp.full((1,), n_blk, jnp.int32)])
    nxt_e = jnp.where(nxt_idx < n_blk, blk_e[jnp.minimum(nxt_idx, n_blk - 1)], -1).astype(jnp.int32)
    return dest.astype(jnp.int32), src, blk_e, n_used.reshape(1), nxt_e, cnt


def _pad_seq(x, t_pad):
    b, t, w = x.shape
    return x if t == t_pad else jnp.pad(x, ((0, 0), (0, t_pad - t), (0, 0)))


def kernel(x_prompt, x_sample, mem_prompt, state_gla, state_ssd, state_conv, cache_mem_k, cache_mem_v,
           norm_mix, w_in, gla_gate_w, gla_gate_b, gla_norm, ssd_conv_w, ssd_conv_b, ssd_dt_bias,
           ssd_a_log, ssd_d, ssd_norm, w_proj_gla, w_proj_ssd, w_out, norm_cross, norm_mem, w_cq,
           w_ck, w_cv, w_co, norm_ffn, router_w, router_b, moe_w1, moe_b1, moe_w2, moe_b2, norm_final):
    assert w_in.shape[0] == 1, "single layer"
    bp, tp, d = x_prompt.shape
    bs, ts, _ = x_sample.shape
    n_p, n_s = bp * tp, bs * ts
    n_all = n_p + n_s

    dv = gla_norm.shape[-1]
    val_w = w_proj_gla.shape[1]
    g_heads = val_w // dv
    key_w = gla_gate_w.shape[-1]
    rank = gla_gate_w.shape[1]
    inner = ssd_norm.shape[-1]
    s_heads = ssd_a_log.shape[-1]
    conv_dim = ssd_conv_w.shape[-1]
    n_state = state_ssd.shape[-1]
    s_groups = (conv_dim - inner) // (2 * n_state)
    kw = ssd_conv_w.shape[1]
    mem_heads, mem_hd = cache_mem_k.shape[-2:]
    mem_w = mem_heads * mem_hd
    n_exp = router_w.shape[-1]
    widths = (key_w, key_w, val_w, val_w, rank, inner, conv_dim, s_heads, d, d)
    offs = [0]
    for w_ in widths:
        offs.append(offs[-1] + w_)
    o_q, o_k, o_v, o_r, o_lr, o_z, o_xbc, o_dt, o_ga, o_gb, o_end = offs
    assert key_w == val_w // 2 and inner == d and val_w == d

    w_in_t = w_in.reshape(w_in.shape[1:]).T
    x_all = jnp.concatenate([x_prompt.reshape(n_p, d), x_sample.reshape(n_s, d)], axis=0)

    h = _rmsnorm(x_all, norm_mix[0], BF16)
    proj_a = _matmul_t(h, w_in_t, row0=0, nrows=o_lr)
    proj_b = _matmul_t(h, w_in_t, row0=o_z, nrows=o_dt - o_z)
    proj_c = _matmul_t(h, w_in_t, row0=o_ga, nrows=o_end - o_ga)
    small_wt = jnp.concatenate(
        [w_in_t[o_dt:o_ga], w_in_t[o_lr:o_z],
         jnp.zeros((V7X_LANES - s_heads - rank, d), F32)], axis=0)
    proj_s = _matmul_t(h, small_wt, row0=0, nrows=V7X_LANES)
    dt_off, lr_off = 0, s_heads

    gate_w, gate_b = gla_gate_w[0], gla_gate_b[0]
    t_pad = -(-ts // V7X_SUBLANES) * V7X_SUBLANES

    def sample_view(arr):
        return _pad_seq(arr[n_p:].reshape(bs, ts, arr.shape[1]), t_pad)

    kb = key_w
    gla_src_p = {"q": (proj_a, kb, 0), "k": (proj_a, kb, 1), "v": (proj_a, val_w, o_v // val_w),
                 "r": (proj_a, val_w, o_r // val_w), "lr": (proj_s, V7X_LANES, 0)}
    c_p = math.gcd(tp, GLA_CHUNK)
    o_p, gla_p = _gla_call(gla_src_p, gate_w, gate_b, gla_norm[0], None, heads=g_heads, batch=bp,
                           n_chunks=tp // c_p, c=c_p, t_valid=c_p, lr_off=lr_off, out_rows=n_all)
    pa_s, ps_s = sample_view(proj_a), sample_view(proj_s)
    gla_src_s = {"q": (pa_s, kb, 0), "k": (pa_s, kb, 1), "v": (pa_s, val_w, o_v // val_w),
                 "r": (pa_s, val_w, o_r // val_w), "lr": (ps_s, V7X_LANES, 0)}
    o_s, gla_s = _gla_call(gla_src_s, gate_w, gate_b, gla_norm[0], state_gla[0], heads=g_heads, batch=bs,
                           n_chunks=1, c=t_pad, t_valid=ts, lr_off=lr_off, out_rows=None)
    o_all = lax.dynamic_update_slice(o_p, o_s[:, :ts].reshape(n_s, val_w), (n_p, 0))

    bc_w = conv_dim - inner
    ssd_src_p = {"x": (proj_b, inner, 1), "bc": (proj_b, bc_w, (2 * inner) // bc_w),
                 "z": (proj_b, inner, 0), "dt": (proj_s, V7X_LANES, 0)}
    cs_p = math.gcd(tp, SSD_CHUNK)
    y_p, ssd_p, xbc_p = _ssd_call(
        ssd_src_p, ssd_conv_w[0], ssd_conv_b[0], ssd_dt_bias[0], ssd_a_log[0], ssd_d[0],
        ssd_norm[0], None, None, heads=s_heads, groups=s_groups, n=n_state, batch=bp,
        n_chunks=tp // cs_p, c=cs_p, t_valid=cs_p, dt_off=dt_off, out_rows=n_all)
    pb_s = sample_view(proj_b)
    ssd_src_s = {"x": (pb_s, inner, 1), "bc": (pb_s, bc_w, (2 * inner) // bc_w),
                 "z": (pb_s, inner, 0), "dt": (ps_s, V7X_LANES, 0)}
    y_s, ssd_s, xbc_s = _ssd_call(
        ssd_src_s, ssd_conv_w[0], ssd_conv_b[0], ssd_dt_bias[0], ssd_a_log[0], ssd_d[0],
        ssd_norm[0], state_conv[0], state_ssd[0], heads=s_heads, groups=s_groups,
        n=n_state, batch=bs, n_chunks=1, c=t_pad, t_valid=ts, dt_off=dt_off, out_rows=None)
    y_all = lax.dynamic_update_slice(y_p, y_s[:, :ts].reshape(n_s, inner), (n_p, 0))

    merged = _merge(o_all, y_all, w_proj_gla[0], w_proj_ssd[0], proj_c)
    x1 = _matmul(merged, w_out[0], res=x_all)

    h2 = _rmsnorm(x1, norm_cross[0], BF16)
    q_all = _matmul(h2, w_cq[0])
    mem = _rmsnorm(mem_prompt.reshape(-1, d), norm_mem[0], BF16)
    mk = _matmul(mem, w_ck[0])
    mv = _matmul(mem, w_cv[0])
    m_len = mem_prompt.shape[1]
    a_p = _attention(q_all[:n_p].reshape(bp, tp, mem_w), mk.reshape(bp, m_len, mem_w),
                     mv.reshape(bp, m_len, mem_w), heads=mem_heads)
    q_s = _pad_seq(q_all[n_p:].reshape(bs, ts, mem_w), t_pad)
    a_s = _attention(q_s, cache_mem_k[0].reshape(bs, -1, mem_w), cache_mem_v[0].reshape(bs, -1, mem_w),
                     heads=mem_heads)
    a_all = jnp.concatenate([a_p.reshape(n_p, mem_w), a_s[:, :ts].reshape(n_s, mem_w)], axis=0)
    x2 = _matmul(a_all, w_co[0], res=x1)

    tm_e = 512
    h3, top_pad, gate_pad = _router(x2, norm_ffn[0], router_w[0], router_b[0])
    dest, src, blk_e, n_used, nxt_e, cnt = _moe_layout(top_pad[:, :TOP_K], n_exp, tm_e)
    rows = _gather_rows(h3, src, n_used * tm_e, d)
    ys = _expert_ffn(rows, blk_e, n_used, nxt_e, cnt, moe_w1[0], moe_b1[0], moe_w2[0], moe_b2[0], tm=tm_e)
    y_fin = _combine(ys, dest, gate_pad, x2, norm_final)

    y_prompt = y_fin[:n_p].reshape(bp, tp, d)
    y_sample = y_fin[n_p:].reshape(bs, ts, d)
    return (y_prompt, y_sample, gla_p[None], ssd_p[None], xbc_p[None],
            mk.reshape(1, bp, m_len, mem_heads, mem_hd), mv.reshape(1, bp, m_len, mem_heads, mem_hd),
            gla_s[None], ssd_s[None], xbc_s[None])
```

```python
import functools
import math

import jax
import jax.numpy as jnp
from jax import lax
from jax.experimental import pallas as pl
from jax.experimental.pallas import tpu as pltpu

F32 = jnp.float32
BF16 = jnp.bfloat16

RMS_EPS = 1e-6
GLA_GATE_TAU = 16.0
GLA_CHUNK = 64
SSD_CHUNK = 64
TOP_K = 4
SWIGLU_LIMIT = 7.0
SWIGLU_ALPHA = 1.702

V7X_LANES = 128
V7X_SUBLANES = 8
V7X_VMEM_BYTES = 64 * 1024 * 1024
VMEM_LIMIT = V7X_VMEM_BYTES - 8 * 1024 * 1024

NT_DIMS = (((1,), (1,)), ((), ()))
TN_DIMS = (((0,), (0,)), ((), ()))


def _cparams(*sem):
    return pltpu.CompilerParams(dimension_semantics=sem, vmem_limit_bytes=VMEM_LIMIT)


def _tile(n, pref, quantum):
    t = (min(pref, n) // quantum) * quantum
    while t >= quantum:
        if n % t == 0:
            return t
        t -= quantum
    return n


def _dot(a, b):
    return jnp.dot(a.astype(BF16), b.astype(BF16), preferred_element_type=F32)


def _dot_nt(a, b):
    return lax.dot_general(a.astype(BF16), b.astype(BF16), NT_DIMS, preferred_element_type=F32)


def _dot_tn(a, b):
    return lax.dot_general(a.astype(BF16), b.astype(BF16), TN_DIMS, preferred_element_type=F32)


def _split3(x):
    hi = x.astype(BF16)
    r1 = x - hi.astype(F32)
    mid = r1.astype(BF16)
    lo = (r1 - mid.astype(F32)).astype(BF16)
    return hi, mid, lo


def _cumsum_rows(x, tri_bf):
    hi, mid, lo = _split3(x)
    n = x.shape[1]
    cat = jnp.concatenate([hi, mid, lo], axis=1)
    s = jnp.dot(tri_bf, cat, preferred_element_type=F32)
    return s[:, :n] + s[:, n:2 * n] + s[:, 2 * n:]


def _softplus(x):
    return jnp.maximum(x, 0.0) + jnp.log1p(jnp.exp(-jnp.abs(x)))


def _silu(x):
    return x * jax.nn.sigmoid(x)


def _tri_masks(c):
    row = lax.broadcasted_iota(jnp.int32, (c, c), 0)
    col = lax.broadcasted_iota(jnp.int32, (c, c), 1)
    mask = row >= col
    return mask, jnp.where(mask, 1.0, 0.0).astype(BF16)


def _rms_kernel(x_ref, g_ref, o_ref):
    x = x_ref[...]
    ms = jnp.mean(x * x, axis=-1, keepdims=True)
    o_ref[...] = (x * lax.rsqrt(ms + RMS_EPS) * g_ref[...]).astype(o_ref.dtype)


def _rmsnorm(x, g, out_dtype):
    n, d = x.shape
    tm = _tile(n, 256, V7X_SUBLANES)
    return pl.pallas_call(
        _rms_kernel,
        grid=(n // tm,),
        in_specs=[pl.BlockSpec((tm, d), lambda i: (i, 0)),
                  pl.BlockSpec((1, d), lambda i: (0, 0))],
        out_specs=pl.BlockSpec((tm, d), lambda i: (i, 0)),
        out_shape=jax.ShapeDtypeStruct((n, d), out_dtype),
        compiler_params=_cparams("parallel"),
        name="rmsnorm",
    )(x, g.reshape(1, d))


def _mm_kernel(*refs, has_res):
    if has_res:
        x_ref, w_ref, r_ref, o_ref = refs
    else:
        x_ref, w_ref, o_ref = refs
        r_ref = None
    acc = jnp.dot(x_ref[...], w_ref[...], preferred_element_type=F32)
    if r_ref is not None:
        acc = acc + r_ref[...]
    o_ref[...] = acc.astype(o_ref.dtype)


def _matmul(x, w, *, col0=0, ncols=None, res=None, out_dtype=F32, tm=512, tn=512):
    n, k = x.shape
    ncols = w.shape[1] - col0 if ncols is None else ncols
    tm = _tile(n, tm, V7X_SUBLANES)
    tn = _tile(math.gcd(ncols, col0) if col0 else ncols, tn, V7X_LANES)
    assert ncols % tn == 0 and col0 % tn == 0
    jb = col0 // tn
    in_specs = [pl.BlockSpec((tm, k), lambda j, i: (i, 0)),
                pl.BlockSpec((k, tn), lambda j, i: (0, j + jb))]
    args = [x, w]
    if res is not None:
        in_specs.append(pl.BlockSpec((tm, tn), lambda j, i: (i, j)))
        args.append(res)
    return pl.pallas_call(
        functools.partial(_mm_kernel, has_res=res is not None),
        grid=(ncols // tn, n // tm),
        in_specs=in_specs,
        out_specs=pl.BlockSpec((tm, tn), lambda j, i: (i, j)),
        out_shape=jax.ShapeDtypeStruct((n, ncols), out_dtype),
        compiler_params=_cparams("arbitrary", "arbitrary"),
        name="matmul",
    )(*args)


def _mmt_kernel(x_ref, wt_hbm, o_ref, wland, wb_ref, sem, *, row0, tn):
    j = pl.program_id(0)

    def tile_copy(jj):
        rows = pl.ds(pl.multiple_of(row0 + jj * tn, V7X_SUBLANES), tn)
        return pltpu.make_async_copy(wt_hbm.at[rows, :], wland, sem)

    @pl.when(pl.program_id(1) == 0)
    def _():
        @pl.when(j == 0)
        def _():
            tile_copy(0).start()

        tile_copy(j).wait()
        wb_ref[...] = wland[...].astype(BF16)

        @pl.when(j + 1 < pl.num_programs(0))
        def _():
            tile_copy(j + 1).start()

    o_ref[...] = lax.dot_general(x_ref[...], wb_ref[...], NT_DIMS, preferred_element_type=F32)


def _matmul_t(x, wt, *, row0, nrows, tm=512, tn=1024):
    n, k = x.shape
    tm = _tile(n, tm, V7X_SUBLANES)
    tn = _tile(nrows, tn, V7X_LANES)
    assert row0 % V7X_SUBLANES == 0 and nrows % tn == 0
    return pl.pallas_call(
        functools.partial(_mmt_kernel, row0=row0, tn=tn),
        grid=(nrows // tn, n // tm),
        in_specs=[pl.BlockSpec((tm, k), lambda j, i: (i, 0)),
                  pl.BlockSpec(memory_space=pl.ANY)],
        out_specs=pl.BlockSpec((tm, tn), lambda j, i: (i, j)),
        out_shape=jax.ShapeDtypeStruct((n, nrows), F32),
        scratch_shapes=[pltpu.VMEM((tn, k), F32), pltpu.VMEM((tn, k), BF16),
                        pltpu.SemaphoreType.DMA(())],
        compiler_params=_cparams("arbitrary", "arbitrary"),
        name="matmul_t",
    )(x, wt)


def _merge_kernel(o_ref, y_ref, wg_hbm, ws_hbm, ga_ref, gb_ref, out_ref, wland, wgb_ref, wsb_ref, sem, *, tn):
    j = pl.program_id(0)

    def tile_copies(jj):
        cols = pl.ds(pl.multiple_of(jj * tn, tn), tn)
        return [pltpu.make_async_copy(w.at[:, cols], wland.at[s], sem.at[s])
                for s, w in enumerate((wg_hbm, ws_hbm))]

    @pl.when(pl.program_id(1) == 0)
    def _():
        @pl.when(j == 0)
        def _():
            for cp in tile_copies(0):
                cp.start()

        for cp in tile_copies(j):
            cp.wait()
        wgb_ref[...] = wland[0].astype(BF16)
        wsb_ref[...] = wland[1].astype(BF16)

        @pl.when(j + 1 < pl.num_programs(0))
        def _():
            for cp in tile_copies(j + 1):
                cp.start()

    pg = jnp.dot(o_ref[...], wgb_ref[...], preferred_element_type=F32)
    ps = jnp.dot(y_ref[...], wsb_ref[...], preferred_element_type=F32)
    out = jax.nn.sigmoid(ga_ref[...]) * pg + jax.nn.sigmoid(gb_ref[...]) * ps
    out_ref[...] = out.astype(out_ref.dtype)


def _merge(o_gla, y_ssd, w_pg, w_ps, gates, tm=256, tn=512):
    n, k = o_gla.shape
    d = w_pg.shape[1]
    tm = _tile(n, tm, V7X_SUBLANES)
    tn = _tile(d, tn, V7X_LANES)
    nj = d // tn
    return pl.pallas_call(
        functools.partial(_merge_kernel, tn=tn),
        grid=(nj, n // tm),
        in_specs=[pl.BlockSpec((tm, k), lambda j, i: (i, 0)),
                  pl.BlockSpec((tm, k), lambda j, i: (i, 0)),
                  pl.BlockSpec(memory_space=pl.ANY),
                  pl.BlockSpec(memory_space=pl.ANY),
                  pl.BlockSpec((tm, tn), lambda j, i: (i, j)),
                  pl.BlockSpec((tm, tn), lambda j, i: (i, j + nj))],
        out_specs=pl.BlockSpec((tm, tn), lambda j, i: (i, j)),
        out_shape=jax.ShapeDtypeStruct((n, d), BF16),
        scratch_shapes=[pltpu.VMEM((2, k, tn), F32), pltpu.VMEM((k, tn), BF16), pltpu.VMEM((k, tn), BF16),
                        pltpu.SemaphoreType.DMA((2,))],
        compiler_params=_cparams("arbitrary", "arbitrary"),
        name="merge",
    )(o_gla, y_ssd, w_pg, w_ps, gates, gates)


def _gla_kernel(*refs, heads, dk, dv, c, t_valid, lr_off, rank, has_s0):
    if has_s0:
        q_ref, k_ref, v_ref, r_ref, lr_ref, gw_ref, gb_ref, gn_ref, s0_ref, o_ref, so_ref, s_scr = refs
    else:
        q_ref, k_ref, v_ref, r_ref, lr_ref, gw_ref, gb_ref, gn_ref, o_ref, so_ref, s_scr = refs
        s0_ref = None
    ci = pl.program_id(1)

    @pl.when(ci == 0)
    def _():
        if s0_ref is None:
            s_scr[...] = jnp.zeros_like(s_scr)
        else:
            s_scr[...] = s0_ref[...]

    mask, tri_bf = _tri_masks(c)
    kw = heads * dk
    scale = dk ** -0.5
    gn = gn_ref[...]
    lr = lr_ref[:, lr_off:lr_off + rank]
    zg = jnp.dot(lr, gw_ref[...], preferred_element_type=F32) + gb_ref[...]
    la = (jnp.minimum(zg, 0.0) - jnp.log1p(jnp.exp(-jnp.abs(zg)))) * (1.0 / GLA_GATE_TAU)
    if t_valid < c:
        la = jnp.where(lax.broadcasted_iota(jnp.int32, (c, kw), 0) < t_valid, la, 0.0)
    b = _cumsum_rows(la, tri_bf)
    k_all = k_ref[...]
    qe = ((q_ref[...] * scale) * jnp.exp(b)).astype(BF16)
    ke = (k_all * jnp.exp(-b)).astype(BF16)
    b_last = b[c - 1:c, :]
    kd = (k_all * jnp.exp(b_last - b)).astype(BF16)
    e_hi, e_mid, e_lo = (t.astype(F32) for t in _split3(jnp.exp(b_last)))
    row8 = lax.broadcasted_iota(jnp.int32, (V7X_SUBLANES, kw), 0)
    e3 = jnp.where(row8 == 0, e_hi, jnp.where(row8 == 1, e_mid, jnp.where(row8 == 2, e_lo, 0.0)))
    ones8 = jnp.ones((V7X_SUBLANES, dv), F32)
    atts = []
    for h in range(heads):
        ks = slice(h * dk, (h + 1) * dk)
        att = lax.dot_general(qe[:, ks], ke[:, ks], NT_DIMS, preferred_element_type=F32)
        atts.append(jnp.where(mask, att, 0.0).astype(BF16))
    for h in range(heads):
        ks = slice(h * dk, (h + 1) * dk)
        vs = slice(h * dv, (h + 1) * dv)
        v = v_ref[:, vs].astype(BF16)
        s_old = s_scr[h]
        o = (jnp.dot(atts[h], v, preferred_element_type=F32)
             + jnp.dot(qe[:, ks], s_old.astype(BF16), preferred_element_type=F32))
        e_full = lax.dot_general(e3[:, ks], ones8, TN_DIMS, preferred_element_type=F32)
        s_scr[h] = e_full * s_old + lax.dot_general(kd[:, ks], v, TN_DIMS, preferred_element_type=F32)
        ms = jnp.mean(o * o, axis=-1, keepdims=True)
        og = o * lax.rsqrt(ms + RMS_EPS) * gn * _silu(r_ref[:, vs])
        o_ref[:, vs] = og.astype(o_ref.dtype)

    @pl.when(ci == pl.num_programs(1) - 1)
    def _():
        so_ref[...] = s_scr[...]


def _gla_call(srcs, gate_w, gate_b, gnorm, s0, *, heads, batch, n_chunks, c, t_valid, lr_off, out_rows):
    rank, kw = gate_w.shape
    dk = kw // heads
    dv = gnorm.shape[-1]
    vw = heads * dv

    def spec(name):
        arr, width, cb = srcs[name]
        if arr.ndim == 3:
            return pl.BlockSpec((None, c, width), lambda b, ci: (b, ci, cb))
        return pl.BlockSpec((c, width), lambda b, ci: (b * n_chunks + ci, cb))

    names = ("q", "k", "v", "r", "lr")
    in_specs = [spec(nm) for nm in names] + [
        pl.BlockSpec((rank, kw), lambda b, ci: (0, 0)),
        pl.BlockSpec((1, kw), lambda b, ci: (0, 0)),
        pl.BlockSpec((1, dv), lambda b, ci: (0, 0)),
    ]
    args = [srcs[nm][0] for nm in names] + [gate_w, gate_b.reshape(1, kw), gnorm.reshape(1, dv)]
    if s0 is not None:
        in_specs.append(pl.BlockSpec((None, heads, dk, dv), lambda b, ci: (b, 0, 0, 0)))
        args.append(s0)
    three_d = srcs["q"][0].ndim == 3
    if three_d:
        o_spec = pl.BlockSpec((None, c, vw), lambda b, ci: (b, ci, 0))
        o_shape = jax.ShapeDtypeStruct((batch, n_chunks * c, vw), BF16)
    else:
        o_spec = pl.BlockSpec((c, vw), lambda b, ci: (b * n_chunks + ci, 0))
        o_shape = jax.ShapeDtypeStruct((out_rows, vw), BF16)
    return pl.pallas_call(
        functools.partial(_gla_kernel, heads=heads, dk=dk, dv=dv, c=c, t_valid=t_valid,
                          lr_off=lr_off, rank=rank, has_s0=s0 is not None),
        grid=(batch, n_chunks),
        in_specs=in_specs,
        out_specs=[o_spec, pl.BlockSpec((None, heads, dk, dv), lambda b, ci: (b, 0, 0, 0))],
        out_shape=[o_shape, jax.ShapeDtypeStruct((batch, heads, dk, dv), F32)],
        scratch_shapes=[pltpu.VMEM((heads, dk, dv), F32)],
        compiler_params=_cparams("parallel", "arbitrary"),
        name="gla",
    )(*args)


def _ssd_kernel(*refs, heads, groups, p, n, c, t_valid, dt_off, conv_w, has_state):
    if has_state:
        (x_ref, bc_ref, z_ref, dt_ref, cw_ref, cb_ref, dtb_ref, alog_ref, dexp_ref, nrm_ref,
         cs_ref, h0_ref, y_ref, ho_ref, cso_ref, h_scr, xf_scr, bcf_scr, y_scr) = refs
    else:
        (x_ref, bc_ref, z_ref, dt_ref, cw_ref, cb_ref, dtb_ref, alog_ref, dexp_ref, nrm_ref,
         y_ref, ho_ref, cso_ref, h_scr, xf_scr, bcf_scr, y_scr) = refs
        cs_ref = h0_ref = None
    ci = pl.program_id(1)
    inner = heads * p
    gn = groups * n
    r_per_g = heads // groups
    pad = V7X_SUBLANES
    hist = conv_w - 1

    @pl.when(ci == 0)
    def _():
        if has_state:
            h_scr[...] = h0_ref[...]
            xf_scr[0:pad, :] = jnp.zeros((pad, inner), F32)
            bcf_scr[0:pad, :] = jnp.zeros((pad, 2 * gn), F32)
            xf_scr[pad - hist:pad, :] = cs_ref[:, 0:inner]
            bcf_scr[pad - hist:pad, :] = cs_ref[:, inner:inner + 2 * gn]
        else:
            h_scr[...] = jnp.zeros_like(h_scr)
            xf_scr[0:pad, :] = jnp.zeros((pad, inner), F32)
            bcf_scr[0:pad, :] = jnp.zeros((pad, 2 * gn), F32)

    xf_scr[pad:pad + c, :] = x_ref[...]
    bcf_scr[pad:pad + c, :] = bc_ref[...]
    xc = cb_ref[:, 0:inner] + xf_scr[pad - hist:pad - hist + c, :] * cw_ref[0:1, 0:inner]
    bcc = cb_ref[:, inner:] + bcf_scr[pad - hist:pad - hist + c, :] * cw_ref[0:1, inner:]
    for j in range(1, conv_w):
        xc = xc + xf_scr[pad - hist + j:pad - hist + j + c, :] * cw_ref[j:j + 1, 0:inner]
        bcc = bcc + bcf_scr[pad - hist + j:pad - hist + j + c, :] * cw_ref[j:j + 1, inner:]
    xs = _silu(xc)
    bcs = _silu(bcc)
    xf_scr[0:pad, :] = x_ref[c - pad:c, :]
    bcf_scr[0:pad, :] = bc_ref[c - pad:c, :]

    dt = _softplus(dt_ref[:, dt_off:dt_off + heads] + dtb_ref[...])
    if t_valid < c:
        dt = jnp.where(lax.broadcasted_iota(jnp.int32, (c, heads), 0) < t_valid, dt, 0.0)
    a_neg = -jnp.exp(alog_ref[...])
    mask, tri_bf = _tri_masks(c)
    cum = _cumsum_rows(dt * a_neg, tri_bf)
    sq = V7X_LANES
    both = jnp.concatenate([cum, dt], axis=1)
    if both.shape[1] < sq:
        both = jnp.concatenate([both, jnp.zeros((c, sq - both.shape[1]), F32)], axis=1)
    if c < sq:
        both = jnp.concatenate([both, jnp.zeros((sq - c, sq), F32)], axis=0)
    both_t = both.T
    e_cum = jnp.exp(cum)
    last = cum[c - 1:c, :]
    wk = jnp.exp(last - cum) * dt
    e_last = jnp.exp(last)

    paired = c == p and 2 * p == V7X_LANES and r_per_g % 2 == 0
    if paired:
        lane2 = lax.broadcasted_iota(jnp.int32, (c, 2 * p), 1)
        lo = lane2 < p
        mask2 = lax.broadcasted_iota(jnp.int32, (c, 2 * p), 0) >= jnp.where(lo, lane2, lane2 - p)
    for g in range(groups):
        bg = bcs[:, g * n:(g + 1) * n]
        cg = bcs[:, gn + g * n:gn + (g + 1) * n]
        cbm = _dot_nt(cg, bg)
        hg = h_scr[g]
        ych = _dot_nt(cg, hg)
        xw_parts = []
        if paired:
            cb2 = jnp.concatenate([cbm, cbm], axis=1)
            for pr in range(r_per_g // 2):
                ha = g * r_per_g + 2 * pr
                hb = ha + 1
                lanes = slice(ha * p, (ha + 2) * p)

                def both(arr, ha=ha, hb=hb):
                    return jnp.where(lo, arr[:, ha:ha + 1], arr[:, hb:hb + 1])

                def rows(r0, ha=ha, hb=hb):
                    return jnp.concatenate([both_t[r0 + ha:r0 + ha + 1, 0:c],
                                            both_t[r0 + hb:r0 + hb + 1, 0:c]], axis=1)

                decay = jnp.exp(jnp.where(mask2, both(cum) - rows(0), -jnp.inf))
                w = cb2 * decay * rows(heads)
                xp = xs[:, lanes]
                x2 = jnp.concatenate([jnp.where(lo, xp, 0.0), jnp.where(lo, 0.0, xp)], axis=0)
                y_scr[:, lanes] = _dot(w, x2) + ych[:, 2 * pr * p:(2 * pr + 2) * p] * both(e_cum)
                xw_parts.append(xp * both(wk))
        for r in range(0 if paired else r_per_g):
            hd = g * r_per_g + r
            cum_col = cum[:, hd:hd + 1]
            cum_row = both_t[hd:hd + 1, 0:c]
            dt_row = both_t[heads + hd:heads + hd + 1, 0:c]
            decay = jnp.exp(jnp.where(mask, cum_col - cum_row, -jnp.inf))
            w = cbm * decay * dt_row
            xh = xs[:, hd * p:(hd + 1) * p]
            yh = _dot(w, xh) + ych[:, r * p:(r + 1) * p] * e_cum[:, hd:hd + 1]
            y_scr[:, hd * p:(hd + 1) * p] = yh
            xw_parts.append(xh * wk[:, hd:hd + 1])
        xw = jnp.concatenate(xw_parts, axis=1)
        upd = _dot_tn(xw, bg)
        for r in range(r_per_g):
            hd = g * r_per_g + r
            rs = slice(r * p, (r + 1) * p)
            h_scr[g, rs, :] = e_last[:, hd:hd + 1] * hg[rs, :] + upd[rs, :]

    y = (y_scr[...] + dexp_ref[...] * xs) * _silu(z_ref[...])
    gw = inner // groups
    for g in range(groups):
        gs = slice(g * gw, (g + 1) * gw)
        yg = y[:, gs]
        ms = jnp.mean(yg * yg, axis=-1, keepdims=True)
        y_ref[:, gs] = (yg * lax.rsqrt(ms + RMS_EPS) * nrm_ref[:, gs]).astype(y_ref.dtype)

    @pl.when(ci == pl.num_programs(1) - 1)
    def _():
        ho_ref[...] = h_scr[...]
        cso_ref[:, 0:inner] = x_ref[t_valid - hist:t_valid, :]
        cso_ref[:, inner:] = bc_ref[t_valid - hist:t_valid, :]


def _ssd_call(srcs, conv_w, conv_b, dt_bias, a_log, d_skip, ssd_norm, conv_state, h0, *,
              heads, groups, n, batch, n_chunks, c, t_valid, dt_off, out_rows):
    kw, conv_dim = conv_w.shape
    inner = ssd_norm.shape[-1]
    p = inner // heads
    gn = groups * n
    r_per_g = heads // groups

    def spec(name):
        arr, width, cb = srcs[name]
        if arr.ndim == 3:
            return pl.BlockSpec((None, c, width), lambda b, ci: (b, ci, cb))
        return pl.BlockSpec((c, width), lambda b, ci: (b * n_chunks + ci, cb))

    def full(shape):
        return pl.BlockSpec(shape, lambda b, ci: (0,) * len(shape))

    names = ("x", "bc", "z", "dt")
    in_specs = [spec(nm) for nm in names] + [
        full((kw, conv_dim)), full((1, conv_dim)), full((1, heads)), full((1, heads)),
        full((1, inner)), full((1, inner))]
    args = [srcs[nm][0] for nm in names] + [
        conv_w, conv_b.reshape(1, conv_dim), dt_bias.reshape(1, heads), a_log.reshape(1, heads),
        jnp.repeat(d_skip, p).reshape(1, inner), ssd_norm.reshape(1, inner)]
    has_state = h0 is not None
    if has_state:
        in_specs += [pl.BlockSpec((None, kw - 1, conv_dim), lambda b, ci: (b, 0, 0)),
                     pl.BlockSpec((None, groups, r_per_g * p, n), lambda b, ci: (b, 0, 0, 0))]
        args += [conv_state, h0.reshape(batch, groups, r_per_g * p, n)]
    three_d = srcs["x"][0].ndim == 3
    if three_d:
        y_spec = pl.BlockSpec((None, c, inner), lambda b, ci: (b, ci, 0))
        y_shape = jax.ShapeDtypeStruct((batch, n_chunks * c, inner), BF16)
    else:
        y_spec = pl.BlockSpec((c, inner), lambda b, ci: (b * n_chunks + ci, 0))
        y_shape = jax.ShapeDtypeStruct((out_rows, inner), BF16)
    assert t_valid >= kw - 1, "the new conv state must lie inside the last chunk"
    y, h_new, conv_new = pl.pallas_call(
        functools.partial(_ssd_kernel, heads=heads, groups=groups, p=p, n=n, c=c, t_valid=t_valid,
                          dt_off=dt_off, conv_w=kw, has_state=has_state),
        grid=(batch, n_chunks),
        in_specs=in_specs,
        out_specs=[y_spec, pl.BlockSpec((None, groups, r_per_g * p, n), lambda b, ci: (b, 0, 0, 0)),
                   pl.BlockSpec((None, kw - 1, conv_dim), lambda b, ci: (b, 0, 0))],
        out_shape=[y_shape, jax.ShapeDtypeStruct((batch, groups, r_per_g * p, n), F32),
                   jax.ShapeDtypeStruct((batch, kw - 1, conv_dim), F32)],
        scratch_shapes=[pltpu.VMEM((groups, r_per_g * p, n), F32),
                        pltpu.VMEM((V7X_SUBLANES + c, inner), F32),
                        pltpu.VMEM((V7X_SUBLANES + c, 2 * gn), F32),
                        pltpu.VMEM((c, inner), F32)],
        compiler_params=_cparams("parallel", "arbitrary"),
        name="ssd",
    )(*args)
    return y, h_new.reshape(batch, heads, p, n), conv_new


def _attn_kernel(q_ref, k_ref, v_ref, o_ref, *, heads, hd):
    scale = hd ** -0.5
    for h in range(heads):
        hs = slice(h * hd, (h + 1) * hd)
        s = _dot_nt(q_ref[:, hs], k_ref[:, hs]) * scale
        m = jnp.max(s, axis=-1, keepdims=True)
        e = jnp.exp(s - m)
        prob = e / jnp.sum(e, axis=-1, keepdims=True)
        o_ref[:, hs] = _dot(prob, v_ref[:, hs]).astype(o_ref.dtype)


def _attention(q, mem_k, mem_v, *, heads, tq=512):
    batch, t, w = q.shape
    m = mem_k.shape[1]
    tq = _tile(t, tq, V7X_SUBLANES)
    return pl.pallas_call(
        functools.partial(_attn_kernel, heads=heads, hd=w // heads),
        grid=(batch, t // tq),
        in_specs=[pl.BlockSpec((None, tq, w), lambda b, i: (b, i, 0)),
                  pl.BlockSpec((None, m, w), lambda b, i: (b, 0, 0)),
                  pl.BlockSpec((None, m, w), lambda b, i: (b, 0, 0))],
        out_specs=pl.BlockSpec((None, tq, w), lambda b, i: (b, i, 0)),
        out_shape=jax.ShapeDtypeStruct((batch, t, w), BF16),
        compiler_params=_cparams("parallel", "arbitrary"),
        name="cross_attention",
    )(q, mem_k, mem_v)


def _router_kernel(x_ref, g_ref, rw_ref, rb_ref, h_ref, idx_ref, gate_ref, *, n_exp, top_k):
    x = x_ref[...]
    ms = jnp.mean(x * x, axis=-1, keepdims=True)
    h = x * lax.rsqrt(ms + RMS_EPS) * g_ref[...]
    tm, d = x.shape
    nseg = d // V7X_LANES
    pitch = _token_pitch(d)
    for s in range(nseg):
        h_ref[pl.ds(s, tm, stride=pitch), :] = h[:, s * V7X_LANES:(s + 1) * V7X_LANES]
    h_hi, h_mid, h_lo = _split3(h)
    w_hi, w_mid, w_lo = _split3(rw_ref[...])

    def d(a, b):
        return jnp.dot(a, b, preferred_element_type=F32)

    logits = (d(h_hi, w_hi) + (d(h_hi, w_mid) + d(h_mid, w_hi))
              + (d(h_hi, w_lo) + d(h_mid, w_mid) + d(h_lo, w_hi))) + rb_ref[...]
    lanes = lax.broadcasted_iota(jnp.int32, logits.shape, 1).astype(F32)
    work = logits
    vals, idxs = [], []
    for _ in range(top_k):
        mx = jnp.max(work, axis=-1, keepdims=True)
        ix = jnp.min(jnp.where(work == mx, lanes, float(n_exp)), axis=-1, keepdims=True)
        vals.append(mx)
        idxs.append(ix)
        work = jnp.where(lanes == ix, -jnp.inf, work)
    es = [jnp.exp(v - vals[0]) for v in vals]
    tot = es[0]
    for e in es[1:]:
        tot = tot + e
    out_lanes = lax.broadcasted_iota(jnp.int32, idx_ref.shape, 1)
    idx_out = jnp.zeros(idx_ref.shape, jnp.int32)
    gate_out = jnp.zeros(gate_ref.shape, F32)
    for j in range(top_k):
        idx_out = jnp.where(out_lanes == j, idxs[j].astype(jnp.int32), idx_out)
        gate_out = jnp.where(out_lanes == j, es[j] / tot, gate_out)
    idx_ref[...] = idx_out
    gate_ref[...] = gate_out


def _router(x, g, router_w, router_b):
    n, d = x.shape
    n_exp = router_w.shape[1]
    tm = _tile(n, 256, V7X_SUBLANES)
    nseg = _token_pitch(d)
    return pl.pallas_call(
        functools.partial(_router_kernel, n_exp=n_exp, top_k=TOP_K),
        grid=(n // tm,),
        in_specs=[pl.BlockSpec((tm, d), lambda i: (i, 0)),
                  pl.BlockSpec((1, d), lambda i: (0, 0)),
                  pl.BlockSpec((d, n_exp), lambda i: (0, 0)),
                  pl.BlockSpec((1, n_exp), lambda i: (0, 0))],
        out_specs=[pl.BlockSpec((tm * nseg, V7X_LANES), lambda i: (i, 0)),
                   pl.BlockSpec((tm, V7X_LANES), lambda i: (i, 0)),
                   pl.BlockSpec((tm, V7X_LANES), lambda i: (i, 0))],
        out_shape=[jax.ShapeDtypeStruct((n * nseg, V7X_LANES), F32),
                   jax.ShapeDtypeStruct((n, V7X_LANES), jnp.int32),
                   jax.ShapeDtypeStruct((n, V7X_LANES), F32)],
        compiler_params=_cparams("parallel"),
        name="router",
    )(x, g.reshape(1, d), router_w, router_b.reshape(1, n_exp))


def _row_copy(src_hbm, row, dst, slot, sem):
    return pltpu.make_async_copy(src_hbm.at[pl.ds(row, 1), :], dst.at[pl.ds(slot, 1), :], sem)


def _token_pitch(d):
    return d // V7X_LANES + 1


def _token_copy(h_hbm, tok, buf, slot, sem, nseg, pitch):
    src = h_hbm.at[pl.ds(tok * pitch, nseg), :]
    dst = buf.at[pl.ds(slot * pitch, nseg), :]
    return pltpu.make_async_copy(src, dst, sem)


GATHER_UNROLL = 8


def _gather_kernel(nrows_ref, idx_ref, h_hbm, o_ref, buf, sem, *, tg, nseg, pitch):
    @pl.when(pl.program_id(0) * tg < nrows_ref[0])
    def _():
        def start(r8, carry):
            for u in range(GATHER_UNROLL):
                r = r8 * GATHER_UNROLL + u
                _token_copy(h_hbm, idx_ref[0, 0, r], buf, r, sem, nseg, pitch).start(priority=u % 2)
            return carry

        lax.fori_loop(0, tg // GATHER_UNROLL, start, 0)

        def wait(r, carry):
            _token_copy(h_hbm, 0, buf, r, sem, nseg, pitch).wait()
            return carry

        lax.fori_loop(0, tg, wait, 0, unroll=GATHER_UNROLL)
        for s in range(nseg):
            seg = buf[pl.ds(s, tg, stride=pitch), :]
            o_ref[:, s * V7X_LANES:(s + 1) * V7X_LANES] = seg.astype(o_ref.dtype)


def _gather_rows(h_tok, src, n_rows, d, tg=256):
    m = src.shape[0]
    nseg = d // V7X_LANES
    pitch = _token_pitch(d)
    return pl.pallas_call(
        functools.partial(_gather_kernel, tg=tg, nseg=nseg, pitch=pitch),
        grid_spec=pltpu.PrefetchScalarGridSpec(
            num_scalar_prefetch=1,
            grid=(m // tg,),
            in_specs=[pl.BlockSpec((1, 1, tg), lambda i, nr: (i, 0, 0), memory_space=pltpu.SMEM),
                      pl.BlockSpec(memory_space=pl.ANY)],
            out_specs=pl.BlockSpec((tg, d), lambda i, nr: (i, 0)),
            scratch_shapes=[pltpu.VMEM((tg * pitch, V7X_LANES), F32), pltpu.SemaphoreType.DMA(())]),
        out_shape=jax.ShapeDtypeStruct((m, d), BF16),
        compiler_params=_cparams("arbitrary"),
        name="moe_gather",
    )(n_rows, src.reshape(m // tg, 1, tg), h_tok)


MOE_SUB_ROWS = 128


def _expert_weights(be_ref, nxt_ref, cnt_ref, gid_ref, ng_ref, copies):
    j, i = pl.program_id(0), pl.program_id(1)
    first = jnp.logical_and(cnt_ref[i] > 0,
                            jnp.logical_or(i == 0, be_ref[i] != be_ref[jnp.maximum(i - 1, 0)]))
    slot = (j * ng_ref[0] + gid_ref[i]) % 2

    @pl.when(first)
    def _():
        @pl.when(jnp.logical_and(j == 0, i == 0))
        def _():
            for cp in copies(be_ref[0], 0, slot):
                cp.start()

        for cp in copies(be_ref[i], j, slot):
            cp.wait()
        nxt = nxt_ref[i]

        @pl.when(nxt >= 0)
        def _():
            for cp in copies(nxt, j, 1 - slot):
                cp.start()

        @pl.when(jnp.logical_and(nxt < 0, j + 1 < pl.num_programs(0)))
        def _():
            for cp in copies(be_ref[0], j + 1, 1 - slot):
                cp.start()

    return slot


def _ffn1_kernel(be_ref, nu_ref, nxt_ref, cnt_ref, gid_ref, ng_ref, x_ref, w1_hbm, bg_ref, bu_ref, o_ref,
                 wland, sem, *, ff, tf):
    def copies(e, jj, slot):
        return [pltpu.make_async_copy(w1_hbm.at[e, :, pl.ds(pl.multiple_of(half * ff + jj * tf, tf), tf)],
                                      wland.at[slot, half], sem.at[slot, half]) for half in range(2)]

    slot = _expert_weights(be_ref, nxt_ref, cnt_ref, gid_ref, ng_ref, copies)
    i = pl.program_id(1)
    for s in range(x_ref.shape[0] // MOE_SUB_ROWS):
        @pl.when(s * MOE_SUB_ROWS < cnt_ref[i])
        def _():
            rs = slice(s * MOE_SUB_ROWS, (s + 1) * MOE_SUB_ROWS)
            x = x_ref[rs, :]
            g_ = jnp.dot(x, wland[slot, 0], preferred_element_type=F32) + bg_ref[...]
            u_ = jnp.dot(x, wland[slot, 1], preferred_element_type=F32) + bu_ref[...]
            g_ = jnp.minimum(g_, SWIGLU_LIMIT)
            u_ = jnp.clip(u_, -SWIGLU_LIMIT, SWIGLU_LIMIT)
            act = (u_ + 1.0) * g_ * jax.nn.sigmoid(SWIGLU_ALPHA * g_)
            o_ref[rs, :] = act.astype(o_ref.dtype)


def _ffn2_kernel(be_ref, nu_ref, nxt_ref, cnt_ref, gid_ref, ng_ref, x_ref, w2_hbm, b_ref, o_ref,
                 wland, sem, *, tn):
    def copies(e, jj, slot):
        return [pltpu.make_async_copy(w2_hbm.at[e, :, pl.ds(pl.multiple_of(jj * tn, tn), tn)],
                                      wland.at[slot], sem.at[slot])]

    slot = _expert_weights(be_ref, nxt_ref, cnt_ref, gid_ref, ng_ref, copies)
    i = pl.program_id(1)
    for s in range(x_ref.shape[0] // MOE_SUB_ROWS):
        @pl.when(s * MOE_SUB_ROWS < cnt_ref[i])
        def _():
            rs = slice(s * MOE_SUB_ROWS, (s + 1) * MOE_SUB_ROWS)
            o_ref[rs, :] = jnp.dot(x_ref[rs, :], wland[slot], preferred_element_type=F32) + b_ref[...]


def _expert_ffn(rows, layout, w1, b1, w2, b2, *, tm, tf=512, tn=1024):
    m_pad, d = rows.shape
    n_exp, _, ff2 = w1.shape
    ff = ff2 // 2
    n_blk = m_pad // tm
    tf = _tile(ff, tf, V7X_LANES)
    tn = _tile(d, tn, V7X_LANES)
    nf = ff // tf
    assert tm % MOE_SUB_ROWS == 0

    def row_blk(i, pf):
        return jnp.minimum(i, pf[1][0] - 1)

    def expert(i, pf):
        return pf[0][i]

    act = pl.pallas_call(
        functools.partial(_ffn1_kernel, ff=ff, tf=tf),
        grid_spec=pltpu.PrefetchScalarGridSpec(
            num_scalar_prefetch=len(layout),
            grid=(nf, n_blk),
            in_specs=[pl.BlockSpec((tm, d), lambda j, i, *pf: (row_blk(i, pf), 0)),
                      pl.BlockSpec(memory_space=pl.ANY),
                      pl.BlockSpec((None, 1, tf), lambda j, i, *pf: (expert(i, pf), 0, j)),
                      pl.BlockSpec((None, 1, tf), lambda j, i, *pf: (expert(i, pf), 0, j + nf))],
            out_specs=pl.BlockSpec((tm, tf), lambda j, i, *pf: (row_blk(i, pf), j)),
            scratch_shapes=[pltpu.VMEM((2, 2, d, tf), F32), pltpu.SemaphoreType.DMA((2, 2))]),
        out_shape=jax.ShapeDtypeStruct((m_pad, ff), BF16),
        compiler_params=_cparams("arbitrary", "arbitrary"),
        name="moe_ffn1",
    )(*layout, rows, w1, b1.reshape(n_exp, 1, ff2), b1.reshape(n_exp, 1, ff2))

    return pl.pallas_call(
        functools.partial(_ffn2_kernel, tn=tn),
        grid_spec=pltpu.PrefetchScalarGridSpec(
            num_scalar_prefetch=len(layout),
            grid=(d // tn, n_blk),
            in_specs=[pl.BlockSpec((tm, ff), lambda j, i, *pf: (row_blk(i, pf), 0)),
                      pl.BlockSpec(memory_space=pl.ANY),
                      pl.BlockSpec((None, 1, tn), lambda j, i, *pf: (expert(i, pf), 0, j))],
            out_specs=pl.BlockSpec((tm, tn), lambda j, i, *pf: (row_blk(i, pf), j)),
            scratch_shapes=[pltpu.VMEM((2, ff, tn), F32), pltpu.SemaphoreType.DMA((2,))]),
        out_shape=jax.ShapeDtypeStruct((m_pad, d), F32),
        compiler_params=_cparams("arbitrary", "arbitrary"),
        name="moe_ffn2",
    )(*layout, act, w2, b2.reshape(n_exp, 1, d))


def _combine_kernel(dest_ref, ys_hbm, gate_ref, x_ref, g_ref, o_ref, buf, sem, *, tc, top_k):
    def start(r, carry):
        for j in range(top_k):
            _row_copy(ys_hbm, dest_ref[0, 0, r * top_k + j], buf.at[j], r, sem).start()
        return carry

    lax.fori_loop(0, tc, start, 0)

    def wait(r, carry):
        for j in range(top_k):
            _row_copy(ys_hbm, 0, buf.at[j], r, sem).wait()
        return carry

    lax.fori_loop(0, tc, wait, 0)
    gate = gate_ref[...]
    y = buf[0] * gate[:, 0:1]
    for j in range(1, top_k):
        y = y + buf[j] * gate[:, j:j + 1]
    x = x_ref[...] + y
    ms = jnp.mean(x * x, axis=-1, keepdims=True)
    o_ref[...] = x * lax.rsqrt(ms + RMS_EPS) * g_ref[...]


def _combine(ys, dest, gate, x, g, tc=128):
    n, d = x.shape
    tc = _tile(n, tc, V7X_SUBLANES)
    return pl.pallas_call(
        functools.partial(_combine_kernel, tc=tc, top_k=TOP_K),
        grid=(n // tc,),
        in_specs=[pl.BlockSpec((1, 1, tc * TOP_K), lambda i: (i, 0, 0), memory_space=pltpu.SMEM),
                  pl.BlockSpec(memory_space=pl.ANY),
                  pl.BlockSpec((tc, V7X_LANES), lambda i: (i, 0)),
                  pl.BlockSpec((tc, d), lambda i: (i, 0)),
                  pl.BlockSpec((1, d), lambda i: (0, 0))],
        out_specs=pl.BlockSpec((tc, d), lambda i: (i, 0)),
        out_shape=jax.ShapeDtypeStruct((n, d), F32),
        scratch_shapes=[pltpu.VMEM((TOP_K, tc, d), F32), pltpu.SemaphoreType.DMA(())],
        compiler_params=_cparams("arbitrary"),
        name="moe_combine",
    )(dest.reshape(n // tc, 1, tc * TOP_K), ys, gate, x, g.reshape(1, d))


def _moe_layout(top_i, n_exp, tm):
    n_tok = top_i.shape[0]
    m = n_tok * TOP_K
    e_flat = top_i.reshape(-1)
    onehot = (e_flat[:, None] == jnp.arange(n_exp, dtype=jnp.int32)[None, :]).astype(jnp.int32)
    rank = jnp.sum((jnp.cumsum(onehot, axis=0) - onehot) * onehot, axis=1)
    counts = jnp.sum(onehot, axis=0)
    pad_counts = (counts + tm - 1) // tm * tm
    pad_end = jnp.cumsum(pad_counts)
    pad_start = pad_end - pad_counts
    dest = pad_start[e_flat] + rank
    n_blk = (m + n_exp * (tm - 1)) // tm + 1
    tok = jnp.arange(m, dtype=jnp.int32) // TOP_K
    src = jnp.zeros((n_blk * tm,), jnp.int32).at[dest].set(tok)
    blk_start = jnp.arange(n_blk, dtype=jnp.int32) * tm
    blk_e = jnp.minimum(jnp.sum(pad_end[None, :] <= blk_start[:, None], axis=1), n_exp - 1).astype(jnp.int32)
    n_used = (pad_end[-1] // tm).astype(jnp.int32)
    blk = jnp.arange(n_blk, dtype=jnp.int32)
    used = blk < n_used
    cnt = jnp.where(used, jnp.clip(counts[blk_e] - (blk_start - pad_start[blk_e]), 0, tm), 0).astype(jnp.int32)
    starts = jnp.logical_and(used, jnp.concatenate([jnp.ones((1,), bool), blk_e[1:] != blk_e[:-1]]))
    start_idx = jnp.where(starts, blk, n_blk)
    nxt_idx = jnp.concatenate([lax.cummin(start_idx, reverse=True)[1:], jnp.full((1,), n_blk, jnp.int32)])
    nxt_e = jnp.where(nxt_idx < n_blk, blk_e[jnp.minimum(nxt_idx, n_blk - 1)], -1).astype(jnp.int32)
    gid = (jnp.cumsum(starts.astype(jnp.int32)) - 1).astype(jnp.int32)
    n_groups = jnp.sum(starts.astype(jnp.int32)).astype(jnp.int32).reshape(1)
    return dest.astype(jnp.int32), src, (blk_e, n_used.reshape(1), nxt_e, cnt, gid, n_groups)


def _pad_seq(x, t_pad):
    b, t, w = x.shape
    return x if t == t_pad else jnp.pad(x, ((0, 0), (0, t_pad - t), (0, 0)))


def kernel(x_prompt, x_sample, mem_prompt, state_gla, state_ssd, state_conv, cache_mem_k, cache_mem_v,
           norm_mix, w_in, gla_gate_w, gla_gate_b, gla_norm, ssd_conv_w, ssd_conv_b, ssd_dt_bias,
           ssd_a_log, ssd_d, ssd_norm, w_proj_gla, w_proj_ssd, w_out, norm_cross, norm_mem, w_cq,
           w_ck, w_cv, w_co, norm_ffn, router_w, router_b, moe_w1, moe_b1, moe_w2, moe_b2, norm_final):
    assert w_in.shape[0] == 1, "single layer"
    bp, tp, d = x_prompt.shape
    bs, ts, _ = x_sample.shape
    n_p, n_s = bp * tp, bs * ts
    n_all = n_p + n_s

    dv = gla_norm.shape[-1]
    val_w = w_proj_gla.shape[1]
    g_heads = val_w // dv
    key_w = gla_gate_w.shape[-1]
    rank = gla_gate_w.shape[1]
    inner = ssd_norm.shape[-1]
    s_heads = ssd_a_log.shape[-1]
    conv_dim = ssd_conv_w.shape[-1]
    n_state = state_ssd.shape[-1]
    s_groups = (conv_dim - inner) // (2 * n_state)
    kw = ssd_conv_w.shape[1]
    mem_heads, mem_hd = cache_mem_k.shape[-2:]
    mem_w = mem_heads * mem_hd
    n_exp = router_w.shape[-1]
    widths = (key_w, key_w, val_w, val_w, rank, inner, conv_dim, s_heads, d, d)
    offs = [0]
    for w_ in widths:
        offs.append(offs[-1] + w_)
    o_q, o_k, o_v, o_r, o_lr, o_z, o_xbc, o_dt, o_ga, o_gb, o_end = offs
    assert key_w == val_w // 2 and inner == d and val_w == d

    w_in_t = w_in.reshape(w_in.shape[1:]).T
    x_all = jnp.concatenate([x_prompt.reshape(n_p, d), x_sample.reshape(n_s, d)], axis=0)

    h = _rmsnorm(x_all, norm_mix[0], BF16)
    proj_a = _matmul_t(h, w_in_t, row0=0, nrows=o_lr)
    proj_b = _matmul_t(h, w_in_t, row0=o_z, nrows=o_dt - o_z)
    proj_c = _matmul_t(h, w_in_t, row0=o_ga, nrows=o_end - o_ga)
    small_wt = jnp.concatenate(
        [w_in_t[o_dt:o_ga], w_in_t[o_lr:o_z],
         jnp.zeros((V7X_LANES - s_heads - rank, d), F32)], axis=0)
    proj_s = _matmul_t(h, small_wt, row0=0, nrows=V7X_LANES)
    dt_off, lr_off = 0, s_heads

    gate_w, gate_b = gla_gate_w[0], gla_gate_b[0]
    t_pad = -(-ts // V7X_SUBLANES) * V7X_SUBLANES

    def sample_view(arr):
        return _pad_seq(arr[n_p:].reshape(bs, ts, arr.shape[1]), t_pad)

    kb = key_w
    gla_src_p = {"q": (proj_a, kb, 0), "k": (proj_a, kb, 1), "v": (proj_a, val_w, o_v // val_w),
                 "r": (proj_a, val_w, o_r // val_w), "lr": (proj_s, V7X_LANES, 0)}
    c_p = math.gcd(tp, GLA_CHUNK)
    o_p, gla_p = _gla_call(gla_src_p, gate_w, gate_b, gla_norm[0], None, heads=g_heads, batch=bp,
                           n_chunks=tp // c_p, c=c_p, t_valid=c_p, lr_off=lr_off, out_rows=n_all)
    pa_s, ps_s = sample_view(proj_a), sample_view(proj_s)
    gla_src_s = {"q": (pa_s, kb, 0), "k": (pa_s, kb, 1), "v": (pa_s, val_w, o_v // val_w),
                 "r": (pa_s, val_w, o_r // val_w), "lr": (ps_s, V7X_LANES, 0)}
    o_s, gla_s = _gla_call(gla_src_s, gate_w, gate_b, gla_norm[0], state_gla[0], heads=g_heads, batch=bs,
                           n_chunks=1, c=t_pad, t_valid=ts, lr_off=lr_off, out_rows=None)
    o_all = lax.dynamic_update_slice(o_p, o_s[:, :ts].reshape(n_s, val_w), (n_p, 0))

    bc_w = conv_dim - inner
    ssd_src_p = {"x": (proj_b, inner, 1), "bc": (proj_b, bc_w, (2 * inner) // bc_w),
                 "z": (proj_b, inner, 0), "dt": (proj_s, V7X_LANES, 0)}
    cs_p = math.gcd(tp, SSD_CHUNK)
    y_p, ssd_p, xbc_p = _ssd_call(
        ssd_src_p, ssd_conv_w[0], ssd_conv_b[0], ssd_dt_bias[0], ssd_a_log[0], ssd_d[0],
        ssd_norm[0], None, None, heads=s_heads, groups=s_groups, n=n_state, batch=bp,
        n_chunks=tp // cs_p, c=cs_p, t_valid=cs_p, dt_off=dt_off, out_rows=n_all)
    pb_s = sample_view(proj_b)
    ssd_src_s = {"x": (pb_s, inner, 1), "bc": (pb_s, bc_w, (2 * inner) // bc_w),
                 "z": (pb_s, inner, 0), "dt": (ps_s, V7X_LANES, 0)}
    y_s, ssd_s, xbc_s = _ssd_call(
        ssd_src_s, ssd_conv_w[0], ssd_conv_b[0], ssd_dt_bias[0], ssd_a_log[0], ssd_d[0],
        ssd_norm[0], state_conv[0], state_ssd[0], heads=s_heads, groups=s_groups,
        n=n_state, batch=bs, n_chunks=1, c=t_pad, t_valid=ts, dt_off=dt_off, out_rows=None)
    y_all = lax.dynamic_update_slice(y_p, y_s[:, :ts].reshape(n_s, inner), (n_p, 0))

    merged = _merge(o_all, y_all, w_proj_gla[0], w_proj_ssd[0], proj_c)
    x1 = _matmul(merged, w_out[0], res=x_all)

    h2 = _rmsnorm(x1, norm_cross[0], BF16)
    q_all = _matmul(h2, w_cq[0])
    mem = _rmsnorm(mem_prompt.reshape(-1, d), norm_mem[0], BF16)
    mk = _matmul(mem, w_ck[0])
    mv = _matmul(mem, w_cv[0])
    m_len = mem_prompt.shape[1]
    a_p = _attention(q_all[:n_p].reshape(bp, tp, mem_w), mk.reshape(bp, m_len, mem_w),
                     mv.reshape(bp, m_len, mem_w), heads=mem_heads)
    q_s = _pad_seq(q_all[n_p:].reshape(bs, ts, mem_w), t_pad)
    a_s = _attention(q_s, cache_mem_k[0].reshape(bs, -1, mem_w), cache_mem_v[0].reshape(bs, -1, mem_w),
                     heads=mem_heads)
    a_all = jnp.concatenate([a_p.reshape(n_p, mem_w), a_s[:, :ts].reshape(n_s, mem_w)], axis=0)
    x2 = _matmul(a_all, w_co[0], res=x1)

    tm_e = 512
    h3, top_pad, gate_pad = _router(x2, norm_ffn[0], router_w[0], router_b[0])
    dest, src, layout = _moe_layout(top_pad[:, :TOP_K], n_exp, tm_e)
    rows = _gather_rows(h3, src, layout[1] * tm_e, d)
    ys = _expert_ffn(rows, layout, moe_w1[0], moe_b1[0], moe_w2[0], moe_b2[0], tm=tm_e)
    y_fin = _combine(ys, dest, gate_pad, x2, norm_final)

    y_prompt = y_fin[:n_p].reshape(bp, tp, d)
    y_sample = y_fin[n_p:].reshape(bs, ts, d)
    return (y_prompt, y_sample, gla_p[None], ssd_p[None], xbc_p[None],
            mk.reshape(1, bp, m_len, mem_heads, mem_hd), mv.reshape(1, bp, m_len, mem_heads, mem_hd),
            gla_s[None], ssd_s[None], xbc_s[None])
```

```python
import functools
import math

import jax
import jax.numpy as jnp
from jax import lax
from jax.experimental import pallas as pl
from jax.experimental.pallas import tpu as pltpu

F32 = jnp.float32
BF16 = jnp.bfloat16

RMS_EPS = 1e-6
GLA_GATE_TAU = 16.0
GLA_CHUNK = 64
SSD_CHUNK = 64
TOP_K = 4
SWIGLU_LIMIT = 7.0
SWIGLU_ALPHA = 1.702

V7X_LANES = 128
V7X_SUBLANES = 8
V7X_VMEM_BYTES = 64 * 1024 * 1024
VMEM_LIMIT = V7X_VMEM_BYTES - 8 * 1024 * 1024

NT_DIMS = (((1,), (1,)), ((), ()))
TN_DIMS = (((0,), (0,)), ((), ()))


def _cparams(*sem):
    return pltpu.CompilerParams(dimension_semantics=sem, vmem_limit_bytes=VMEM_LIMIT)


def _tile(n, pref, quantum):
    t = (min(pref, n) // quantum) * quantum
    while t >= quantum:
        if n % t == 0:
            return t
        t -= quantum
    return n


def _dot(a, b):
    return jnp.dot(a.astype(BF16), b.astype(BF16), preferred_element_type=F32)


def _dot_nt(a, b):
    return lax.dot_general(a.astype(BF16), b.astype(BF16), NT_DIMS, preferred_element_type=F32)


def _dot_tn(a, b):
    return lax.dot_general(a.astype(BF16), b.astype(BF16), TN_DIMS, preferred_element_type=F32)


def _split3(x):
    hi = x.astype(BF16)
    r1 = x - hi.astype(F32)
    mid = r1.astype(BF16)
    lo = (r1 - mid.astype(F32)).astype(BF16)
    return hi, mid, lo


def _cumsum_rows(x, tri_bf):
    hi, mid, lo = _split3(x)
    n = x.shape[1]
    cat = jnp.concatenate([hi, mid, lo], axis=1)
    s = jnp.dot(tri_bf, cat, preferred_element_type=F32)
    return s[:, :n] + s[:, n:2 * n] + s[:, 2 * n:]


def _softplus(x):
    return jnp.maximum(x, 0.0) + jnp.log1p(jnp.exp(-jnp.abs(x)))


def _silu(x):
    return x * jax.nn.sigmoid(x)


def _tri_masks(c):
    row = lax.broadcasted_iota(jnp.int32, (c, c), 0)
    col = lax.broadcasted_iota(jnp.int32, (c, c), 1)
    mask = row >= col
    return mask, jnp.where(mask, 1.0, 0.0).astype(BF16)


def _rms_kernel(x_ref, g_ref, o_ref):
    x = x_ref[...]
    ms = jnp.mean(x * x, axis=-1, keepdims=True)
    o_ref[...] = (x * lax.rsqrt(ms + RMS_EPS) * g_ref[...]).astype(o_ref.dtype)


def _rmsnorm(x, g, out_dtype):
    n, d = x.shape
    tm = _tile(n, 256, V7X_SUBLANES)
    return pl.pallas_call(
        _rms_kernel,
        grid=(n // tm,),
        in_specs=[pl.BlockSpec((tm, d), lambda i: (i, 0)),
                  pl.BlockSpec((1, d), lambda i: (0, 0))],
        out_specs=pl.BlockSpec((tm, d), lambda i: (i, 0)),
        out_shape=jax.ShapeDtypeStruct((n, d), out_dtype),
        compiler_params=_cparams("parallel"),
        name="rmsnorm",
    )(x, g.reshape(1, d))


def _mm_kernel(*refs, has_res):
    if has_res:
        x_ref, w_ref, r_ref, o_ref = refs
    else:
        x_ref, w_ref, o_ref = refs
        r_ref = None
    acc = jnp.dot(x_ref[...], w_ref[...], preferred_element_type=F32)
    if r_ref is not None:
        acc = acc + r_ref[...]
    o_ref[...] = acc.astype(o_ref.dtype)


def _matmul(x, w, *, col0=0, ncols=None, res=None, out_dtype=F32, tm=512, tn=512):
    n, k = x.shape
    ncols = w.shape[1] - col0 if ncols is None else ncols
    tm = _tile(n, tm, V7X_SUBLANES)
    tn = _tile(math.gcd(ncols, col0) if col0 else ncols, tn, V7X_LANES)
    assert ncols % tn == 0 and col0 % tn == 0
    jb = col0 // tn
    in_specs = [pl.BlockSpec((tm, k), lambda j, i: (i, 0)),
                pl.BlockSpec((k, tn), lambda j, i: (0, j + jb))]
    args = [x, w]
    if res is not None:
        in_specs.append(pl.BlockSpec((tm, tn), lambda j, i: (i, j)))
        args.append(res)
    return pl.pallas_call(
        functools.partial(_mm_kernel, has_res=res is not None),
        grid=(ncols // tn, n // tm),
        in_specs=in_specs,
        out_specs=pl.BlockSpec((tm, tn), lambda j, i: (i, j)),
        out_shape=jax.ShapeDtypeStruct((n, ncols), out_dtype),
        compiler_params=_cparams("arbitrary", "arbitrary"),
        name="matmul",
    )(*args)


def _mmt_kernel(x_ref, wt_hbm, o_ref, wland, wb_ref, sem, *, row0, tn):
    j = pl.program_id(0)

    def tile_copy(jj):
        rows = pl.ds(pl.multiple_of(row0 + jj * tn, V7X_SUBLANES), tn)
        return pltpu.make_async_copy(wt_hbm.at[rows, :], wland, sem)

    @pl.when(pl.program_id(1) == 0)
    def _():
        @pl.when(j == 0)
        def _():
            tile_copy(0).start()

        tile_copy(j).wait()
        wb_ref[...] = wland[...].astype(BF16)

        @pl.when(j + 1 < pl.num_programs(0))
        def _():
            tile_copy(j + 1).start()

    o_ref[...] = lax.dot_general(x_ref[...], wb_ref[...], NT_DIMS, preferred_element_type=F32)


def _matmul_t(x, wt, *, row0, nrows, tm=512, tn=1024):
    n, k = x.shape
    tm = _tile(n, tm, V7X_SUBLANES)
    tn = _tile(nrows, tn, V7X_LANES)
    assert row0 % V7X_SUBLANES == 0 and nrows % tn == 0
    return pl.pallas_call(
        functools.partial(_mmt_kernel, row0=row0, tn=tn),
        grid=(nrows // tn, n // tm),
        in_specs=[pl.BlockSpec((tm, k), lambda j, i: (i, 0)),
                  pl.BlockSpec(memory_space=pl.ANY)],
        out_specs=pl.BlockSpec((tm, tn), lambda j, i: (i, j)),
        out_shape=jax.ShapeDtypeStruct((n, nrows), F32),
        scratch_shapes=[pltpu.VMEM((tn, k), F32), pltpu.VMEM((tn, k), BF16),
                        pltpu.SemaphoreType.DMA(())],
        compiler_params=_cparams("arbitrary", "arbitrary"),
        name="matmul_t",
    )(x, wt)


def _merge_kernel(o_ref, y_ref, wg_hbm, ws_hbm, ga_ref, gb_ref, out_ref, wland, wgb_ref, wsb_ref, sem, *, tn):
    j = pl.program_id(0)

    def tile_copies(jj):
        cols = pl.ds(pl.multiple_of(jj * tn, tn), tn)
        return [pltpu.make_async_copy(w.at[:, cols], wland.at[s], sem.at[s])
                for s, w in enumerate((wg_hbm, ws_hbm))]

    @pl.when(pl.program_id(1) == 0)
    def _():
        @pl.when(j == 0)
        def _():
            for cp in tile_copies(0):
                cp.start()

        for cp in tile_copies(j):
            cp.wait()
        wgb_ref[...] = wland[0].astype(BF16)
        wsb_ref[...] = wland[1].astype(BF16)

        @pl.when(j + 1 < pl.num_programs(0))
        def _():
            for cp in tile_copies(j + 1):
                cp.start()

    pg = jnp.dot(o_ref[...], wgb_ref[...], preferred_element_type=F32)
    ps = jnp.dot(y_ref[...], wsb_ref[...], preferred_element_type=F32)
    out = jax.nn.sigmoid(ga_ref[...]) * pg + jax.nn.sigmoid(gb_ref[...]) * ps
    out_ref[...] = out.astype(out_ref.dtype)


def _merge(o_gla, y_ssd, w_pg, w_ps, gates, tm=256, tn=512):
    n, k = o_gla.shape
    d = w_pg.shape[1]
    tm = _tile(n, tm, V7X_SUBLANES)
    tn = _tile(d, tn, V7X_LANES)
    nj = d // tn
    return pl.pallas_call(
        functools.partial(_merge_kernel, tn=tn),
        grid=(nj, n // tm),
        in_specs=[pl.BlockSpec((tm, k), lambda j, i: (i, 0)),
                  pl.BlockSpec((tm, k), lambda j, i: (i, 0)),
                  pl.BlockSpec(memory_space=pl.ANY),
                  pl.BlockSpec(memory_space=pl.ANY),
                  pl.BlockSpec((tm, tn), lambda j, i: (i, j)),
                  pl.BlockSpec((tm, tn), lambda j, i: (i, j + nj))],
        out_specs=pl.BlockSpec((tm, tn), lambda j, i: (i, j)),
        out_shape=jax.ShapeDtypeStruct((n, d), BF16),
        scratch_shapes=[pltpu.VMEM((2, k, tn), F32), pltpu.VMEM((k, tn), BF16), pltpu.VMEM((k, tn), BF16),
                        pltpu.SemaphoreType.DMA((2,))],
        compiler_params=_cparams("arbitrary", "arbitrary"),
        name="merge",
    )(o_gla, y_ssd, w_pg, w_ps, gates, gates)


def _gla_kernel(*refs, heads, dk, dv, c, t_valid, lr_off, rank, has_s0):
    if has_s0:
        q_ref, k_ref, v_ref, r_ref, lr_ref, gw_ref, gb_ref, gn_ref, s0_ref, o_ref, so_ref, s_scr = refs
    else:
        q_ref, k_ref, v_ref, r_ref, lr_ref, gw_ref, gb_ref, gn_ref, o_ref, so_ref, s_scr = refs
        s0_ref = None
    ci = pl.program_id(1)

    @pl.when(ci == 0)
    def _():
        if s0_ref is None:
            s_scr[...] = jnp.zeros_like(s_scr)
        else:
            s_scr[...] = s0_ref[...]

    mask, tri_bf = _tri_masks(c)
    kw = heads * dk
    scale = dk ** -0.5
    gn = gn_ref[...]
    lr = lr_ref[:, lr_off:lr_off + rank]
    zg = jnp.dot(lr, gw_ref[...], preferred_element_type=F32) + gb_ref[...]
    la = (jnp.minimum(zg, 0.0) - jnp.log1p(jnp.exp(-jnp.abs(zg)))) * (1.0 / GLA_GATE_TAU)
    if t_valid < c:
        la = jnp.where(lax.broadcasted_iota(jnp.int32, (c, kw), 0) < t_valid, la, 0.0)
    b = _cumsum_rows(la, tri_bf)
    k_all = k_ref[...]
    qe = ((q_ref[...] * scale) * jnp.exp(b)).astype(BF16)
    ke = (k_all * jnp.exp(-b)).astype(BF16)
    b_last = b[c - 1:c, :]
    kd = (k_all * jnp.exp(b_last - b)).astype(BF16)
    e_hi, e_mid, e_lo = (t.astype(F32) for t in _split3(jnp.exp(b_last)))
    row8 = lax.broadcasted_iota(jnp.int32, (V7X_SUBLANES, kw), 0)
    e3 = jnp.where(row8 == 0, e_hi, jnp.where(row8 == 1, e_mid, jnp.where(row8 == 2, e_lo, 0.0)))
    ones8 = jnp.ones((V7X_SUBLANES, dv), F32)
    atts = []
    for h in range(heads):
        ks = slice(h * dk, (h + 1) * dk)
        att = lax.dot_general(qe[:, ks], ke[:, ks], NT_DIMS, preferred_element_type=F32)
        atts.append(jnp.where(mask, att, 0.0).astype(BF16))
    for h in range(heads):
        ks = slice(h * dk, (h + 1) * dk)
        vs = slice(h * dv, (h + 1) * dv)
        v = v_ref[:, vs].astype(BF16)
        s_old = s_scr[h]
        o = (jnp.dot(atts[h], v, preferred_element_type=F32)
             + jnp.dot(qe[:, ks], s_old.astype(BF16), preferred_element_type=F32))
        e_full = lax.dot_general(e3[:, ks], ones8, TN_DIMS, preferred_element_type=F32)
        s_scr[h] = e_full * s_old + lax.dot_general(kd[:, ks], v, TN_DIMS, preferred_element_type=F32)
        ms = jnp.mean(o * o, axis=-1, keepdims=True)
        og = o * lax.rsqrt(ms + RMS_EPS) * gn * _silu(r_ref[:, vs])
        o_ref[:, vs] = og.astype(o_ref.dtype)

    @pl.when(ci == pl.num_programs(1) - 1)
    def _():
        so_ref[...] = s_scr[...]


def _gla_call(srcs, gate_w, gate_b, gnorm, s0, *, heads, batch, n_chunks, c, t_valid, lr_off, out_rows):
    rank, kw = gate_w.shape
    dk = kw // heads
    dv = gnorm.shape[-1]
    vw = heads * dv

    def spec(name):
        arr, width, cb = srcs[name]
        if arr.ndim == 3:
            return pl.BlockSpec((None, c, width), lambda b, ci: (b, ci, cb))
        return pl.BlockSpec((c, width), lambda b, ci: (b * n_chunks + ci, cb))

    names = ("q", "k", "v", "r", "lr")
    in_specs = [spec(nm) for nm in names] + [
        pl.BlockSpec((rank, kw), lambda b, ci: (0, 0)),
        pl.BlockSpec((1, kw), lambda b, ci: (0, 0)),
        pl.BlockSpec((1, dv), lambda b, ci: (0, 0)),
    ]
    args = [srcs[nm][0] for nm in names] + [gate_w, gate_b.reshape(1, kw), gnorm.reshape(1, dv)]
    if s0 is not None:
        in_specs.append(pl.BlockSpec((None, heads, dk, dv), lambda b, ci: (b, 0, 0, 0)))
        args.append(s0)
    three_d = srcs["q"][0].ndim == 3
    if three_d:
        o_spec = pl.BlockSpec((None, c, vw), lambda b, ci: (b, ci, 0))
        o_shape = jax.ShapeDtypeStruct((batch, n_chunks * c, vw), BF16)
    else:
        o_spec = pl.BlockSpec((c, vw), lambda b, ci: (b * n_chunks + ci, 0))
        o_shape = jax.ShapeDtypeStruct((out_rows, vw), BF16)
    return pl.pallas_call(
        functools.partial(_gla_kernel, heads=heads, dk=dk, dv=dv, c=c, t_valid=t_valid,
                          lr_off=lr_off, rank=rank, has_s0=s0 is not None),
        grid=(batch, n_chunks),
        in_specs=in_specs,
        out_specs=[o_spec, pl.BlockSpec((None, heads, dk, dv), lambda b, ci: (b, 0, 0, 0))],
        out_shape=[o_shape, jax.ShapeDtypeStruct((batch, heads, dk, dv), F32)],
        scratch_shapes=[pltpu.VMEM((heads, dk, dv), F32)],
        compiler_params=_cparams("parallel", "arbitrary"),
        name="gla",
    )(*args)


def _ssd_kernel(*refs, heads, groups, p, n, c, t_valid, dt_off, conv_w, has_state):
    if has_state:
        (x_ref, bc_ref, z_ref, dt_ref, cw_ref, cb_ref, dtb_ref, alog_ref, dexp_ref, nrm_ref,
         cs_ref, h0_ref, y_ref, ho_ref, cso_ref, h_scr, xf_scr, bcf_scr, y_scr) = refs
    else:
        (x_ref, bc_ref, z_ref, dt_ref, cw_ref, cb_ref, dtb_ref, alog_ref, dexp_ref, nrm_ref,
         y_ref, ho_ref, cso_ref, h_scr, xf_scr, bcf_scr, y_scr) = refs
        cs_ref = h0_ref = None
    ci = pl.program_id(1)
    inner = heads * p
    gn = groups * n
    r_per_g = heads // groups
    pad = V7X_SUBLANES
    hist = conv_w - 1

    @pl.when(ci == 0)
    def _():
        if has_state:
            h_scr[...] = h0_ref[...]
            xf_scr[0:pad, :] = jnp.zeros((pad, inner), F32)
            bcf_scr[0:pad, :] = jnp.zeros((pad, 2 * gn), F32)
            xf_scr[pad - hist:pad, :] = cs_ref[:, 0:inner]
            bcf_scr[pad - hist:pad, :] = cs_ref[:, inner:inner + 2 * gn]
        else:
            h_scr[...] = jnp.zeros_like(h_scr)
            xf_scr[0:pad, :] = jnp.zeros((pad, inner), F32)
            bcf_scr[0:pad, :] = jnp.zeros((pad, 2 * gn), F32)

    xf_scr[pad:pad + c, :] = x_ref[...]
    bcf_scr[pad:pad + c, :] = bc_ref[...]
    xc = cb_ref[:, 0:inner] + xf_scr[pad - hist:pad - hist + c, :] * cw_ref[0:1, 0:inner]
    bcc = cb_ref[:, inner:] + bcf_scr[pad - hist:pad - hist + c, :] * cw_ref[0:1, inner:]
    for j in range(1, conv_w):
        xc = xc + xf_scr[pad - hist + j:pad - hist + j + c, :] * cw_ref[j:j + 1, 0:inner]
        bcc = bcc + bcf_scr[pad - hist + j:pad - hist + j + c, :] * cw_ref[j:j + 1, inner:]
    xs = _silu(xc)
    bcs = _silu(bcc)
    xf_scr[0:pad, :] = x_ref[c - pad:c, :]
    bcf_scr[0:pad, :] = bc_ref[c - pad:c, :]

    dt = _softplus(dt_ref[:, dt_off:dt_off + heads] + dtb_ref[...])
    if t_valid < c:
        dt = jnp.where(lax.broadcasted_iota(jnp.int32, (c, heads), 0) < t_valid, dt, 0.0)
    a_neg = -jnp.exp(alog_ref[...])
    mask, tri_bf = _tri_masks(c)
    cum = _cumsum_rows(dt * a_neg, tri_bf)
    sq = V7X_LANES
    both = jnp.concatenate([cum, dt], axis=1)
    if both.shape[1] < sq:
        both = jnp.concatenate([both, jnp.zeros((c, sq - both.shape[1]), F32)], axis=1)
    if c < sq:
        both = jnp.concatenate([both, jnp.zeros((sq - c, sq), F32)], axis=0)
    both_t = both.T
    e_cum = jnp.exp(cum)
    last = cum[c - 1:c, :]
    wk = jnp.exp(last - cum) * dt
    e_last = jnp.exp(last)

    paired = c == p and 2 * p == V7X_LANES and r_per_g % 2 == 0
    if paired:
        lane2 = lax.broadcasted_iota(jnp.int32, (c, 2 * p), 1)
        lo = lane2 < p
        mask2 = lax.broadcasted_iota(jnp.int32, (c, 2 * p), 0) >= jnp.where(lo, lane2, lane2 - p)
    for g in range(groups):
        bg = bcs[:, g * n:(g + 1) * n]
        cg = bcs[:, gn + g * n:gn + (g + 1) * n]
        cbm = _dot_nt(cg, bg)
        hg = h_scr[g]
        ych = _dot_nt(cg, hg)
        xw_parts = []
        if paired:
            cb2 = jnp.concatenate([cbm, cbm], axis=1)
            for pr in range(r_per_g // 2):
                ha = g * r_per_g + 2 * pr
                hb = ha + 1
                lanes = slice(ha * p, (ha + 2) * p)

                def both(arr, ha=ha, hb=hb):
                    return jnp.where(lo, arr[:, ha:ha + 1], arr[:, hb:hb + 1])

                def rows(r0, ha=ha, hb=hb):
                    return jnp.concatenate([both_t[r0 + ha:r0 + ha + 1, 0:c],
                                            both_t[r0 + hb:r0 + hb + 1, 0:c]], axis=1)

                decay = jnp.exp(jnp.where(mask2, both(cum) - rows(0), -jnp.inf))
                w = cb2 * decay * rows(heads)
                xp = xs[:, lanes]
                x2 = jnp.concatenate([jnp.where(lo, xp, 0.0), jnp.where(lo, 0.0, xp)], axis=0)
                y_scr[:, lanes] = _dot(w, x2) + ych[:, 2 * pr * p:(2 * pr + 2) * p] * both(e_cum)
                xw_parts.append(xp * both(wk))
        for r in range(0 if paired else r_per_g):
            hd = g * r_per_g + r
            cum_col = cum[:, hd:hd + 1]
            cum_row = both_t[hd:hd + 1, 0:c]
            dt_row = both_t[heads + hd:heads + hd + 1, 0:c]
            decay = jnp.exp(jnp.where(mask, cum_col - cum_row, -jnp.inf))
            w = cbm * decay * dt_row
            xh = xs[:, hd * p:(hd + 1) * p]
            yh = _dot(w, xh) + ych[:, r * p:(r + 1) * p] * e_cum[:, hd:hd + 1]
            y_scr[:, hd * p:(hd + 1) * p] = yh
            xw_parts.append(xh * wk[:, hd:hd + 1])
        xw = jnp.concatenate(xw_parts, axis=1)
        upd = _dot_tn(xw, bg)
        for r in range(r_per_g):
            hd = g * r_per_g + r
            rs = slice(r * p, (r + 1) * p)
            h_scr[g, rs, :] = e_last[:, hd:hd + 1] * hg[rs, :] + upd[rs, :]

    y = (y_scr[...] + dexp_ref[...] * xs) * _silu(z_ref[...])
    gw = inner // groups
    for g in range(groups):
        gs = slice(g * gw, (g + 1) * gw)
        yg = y[:, gs]
        ms = jnp.mean(yg * yg, axis=-1, keepdims=True)
        y_ref[:, gs] = (yg * lax.rsqrt(ms + RMS_EPS) * nrm_ref[:, gs]).astype(y_ref.dtype)

    @pl.when(ci == pl.num_programs(1) - 1)
    def _():
        ho_ref[...] = h_scr[...]
        cso_ref[:, 0:inner] = x_ref[t_valid - hist:t_valid, :]
        cso_ref[:, inner:] = bc_ref[t_valid - hist:t_valid, :]


def _ssd_call(srcs, conv_w, conv_b, dt_bias, a_log, d_skip, ssd_norm, conv_state, h0, *,
              heads, groups, n, batch, n_chunks, c, t_valid, dt_off, out_rows):
    kw, conv_dim = conv_w.shape
    inner = ssd_norm.shape[-1]
    p = inner // heads
    gn = groups * n
    r_per_g = heads // groups

    def spec(name):
        arr, width, cb = srcs[name]
        if arr.ndim == 3:
            return pl.BlockSpec((None, c, width), lambda b, ci: (b, ci, cb))
        return pl.BlockSpec((c, width), lambda b, ci: (b * n_chunks + ci, cb))

    def full(shape):
        return pl.BlockSpec(shape, lambda b, ci: (0,) * len(shape))

    names = ("x", "bc", "z", "dt")
    in_specs = [spec(nm) for nm in names] + [
        full((kw, conv_dim)), full((1, conv_dim)), full((1, heads)), full((1, heads)),
        full((1, inner)), full((1, inner))]
    args = [srcs[nm][0] for nm in names] + [
        conv_w, conv_b.reshape(1, conv_dim), dt_bias.reshape(1, heads), a_log.reshape(1, heads),
        jnp.repeat(d_skip, p).reshape(1, inner), ssd_norm.reshape(1, inner)]
    has_state = h0 is not None
    if has_state:
        in_specs += [pl.BlockSpec((None, kw - 1, conv_dim), lambda b, ci: (b, 0, 0)),
                     pl.BlockSpec((None, groups, r_per_g * p, n), lambda b, ci: (b, 0, 0, 0))]
        args += [conv_state, h0.reshape(batch, groups, r_per_g * p, n)]
    three_d = srcs["x"][0].ndim == 3
    if three_d:
        y_spec = pl.BlockSpec((None, c, inner), lambda b, ci: (b, ci, 0))
        y_shape = jax.ShapeDtypeStruct((batch, n_chunks * c, inner), BF16)
    else:
        y_spec = pl.BlockSpec((c, inner), lambda b, ci: (b * n_chunks + ci, 0))
        y_shape = jax.ShapeDtypeStruct((out_rows, inner), BF16)
    assert t_valid >= kw - 1, "the new conv state must lie inside the last chunk"
    y, h_new, conv_new = pl.pallas_call(
        functools.partial(_ssd_kernel, heads=heads, groups=groups, p=p, n=n, c=c, t_valid=t_valid,
                          dt_off=dt_off, conv_w=kw, has_state=has_state),
        grid=(batch, n_chunks),
        in_specs=in_specs,
        out_specs=[y_spec, pl.BlockSpec((None, groups, r_per_g * p, n), lambda b, ci: (b, 0, 0, 0)),
                   pl.BlockSpec((None, kw - 1, conv_dim), lambda b, ci: (b, 0, 0))],
        out_shape=[y_shape, jax.ShapeDtypeStruct((batch, groups, r_per_g * p, n), F32),
                   jax.ShapeDtypeStruct((batch, kw - 1, conv_dim), F32)],
        scratch_shapes=[pltpu.VMEM((groups, r_per_g * p, n), F32),
                        pltpu.VMEM((V7X_SUBLANES + c, inner), F32),
                        pltpu.VMEM((V7X_SUBLANES + c, 2 * gn), F32),
                        pltpu.VMEM((c, inner), F32)],
        compiler_params=_cparams("parallel", "arbitrary"),
        name="ssd",
    )(*args)
    return y, h_new.reshape(batch, heads, p, n), conv_new


def _attn_kernel(q_ref, k_ref, v_ref, o_ref, *, heads, hd):
    scale = hd ** -0.5
    for h in range(heads):
        hs = slice(h * hd, (h + 1) * hd)
        s = _dot_nt(q_ref[:, hs], k_ref[:, hs]) * scale
        m = jnp.max(s, axis=-1, keepdims=True)
        e = jnp.exp(s - m)
        prob = e / jnp.sum(e, axis=-1, keepdims=True)
        o_ref[:, hs] = _dot(prob, v_ref[:, hs]).astype(o_ref.dtype)


def _attention(q, mem_k, mem_v, *, heads, tq=512):
    batch, t, w = q.shape
    m = mem_k.shape[1]
    tq = _tile(t, tq, V7X_SUBLANES)
    return pl.pallas_call(
        functools.partial(_attn_kernel, heads=heads, hd=w // heads),
        grid=(batch, t // tq),
        in_specs=[pl.BlockSpec((None, tq, w), lambda b, i: (b, i, 0)),
                  pl.BlockSpec((None, m, w), lambda b, i: (b, 0, 0)),
                  pl.BlockSpec((None, m, w), lambda b, i: (b, 0, 0))],
        out_specs=pl.BlockSpec((None, tq, w), lambda b, i: (b, i, 0)),
        out_shape=jax.ShapeDtypeStruct((batch, t, w), BF16),
        compiler_params=_cparams("parallel", "arbitrary"),
        name="cross_attention",
    )(q, mem_k, mem_v)


def _router_kernel(x_ref, g_ref, rw_ref, rb_ref, h_ref, idx_ref, gate_ref, *, n_exp, top_k):
    x = x_ref[...]
    ms = jnp.mean(x * x, axis=-1, keepdims=True)
    h = x * lax.rsqrt(ms + RMS_EPS) * g_ref[...]
    tm, d = x.shape
    nseg = d // V7X_LANES
    pitch = _token_pitch(d)
    for s in range(nseg):
        h_ref[pl.ds(s, tm, stride=pitch), :] = h[:, s * V7X_LANES:(s + 1) * V7X_LANES]
    h_hi, h_mid, h_lo = _split3(h)
    w_hi, w_mid, w_lo = _split3(rw_ref[...])

    def d(a, b):
        return jnp.dot(a, b, preferred_element_type=F32)

    logits = (d(h_hi, w_hi) + (d(h_hi, w_mid) + d(h_mid, w_hi))
              + (d(h_hi, w_lo) + d(h_mid, w_mid) + d(h_lo, w_hi))) + rb_ref[...]
    lanes = lax.broadcasted_iota(jnp.int32, logits.shape, 1).astype(F32)
    work = logits
    vals, idxs = [], []
    for _ in range(top_k):
        mx = jnp.max(work, axis=-1, keepdims=True)
        ix = jnp.min(jnp.where(work == mx, lanes, float(n_exp)), axis=-1, keepdims=True)
        vals.append(mx)
        idxs.append(ix)
        work = jnp.where(lanes == ix, -jnp.inf, work)
    es = [jnp.exp(v - vals[0]) for v in vals]
    tot = es[0]
    for e in es[1:]:
        tot = tot + e
    out_lanes = lax.broadcasted_iota(jnp.int32, idx_ref.shape, 1)
    idx_out = jnp.zeros(idx_ref.shape, jnp.int32)
    gate_out = jnp.zeros(gate_ref.shape, F32)
    for j in range(top_k):
        idx_out = jnp.where(out_lanes == j, idxs[j].astype(jnp.int32), idx_out)
        gate_out = jnp.where(out_lanes == j, es[j] / tot, gate_out)
    idx_ref[...] = idx_out
    gate_ref[...] = gate_out


def _router(x, g, router_w, router_b):
    n, d = x.shape
    n_exp = router_w.shape[1]
    tm = _tile(n, 256, V7X_SUBLANES)
    nseg = _token_pitch(d)
    return pl.pallas_call(
        functools.partial(_router_kernel, n_exp=n_exp, top_k=TOP_K),
        grid=(n // tm,),
        in_specs=[pl.BlockSpec((tm, d), lambda i: (i, 0)),
                  pl.BlockSpec((1, d), lambda i: (0, 0)),
                  pl.BlockSpec((d, n_exp), lambda i: (0, 0)),
                  pl.BlockSpec((1, n_exp), lambda i: (0, 0))],
        out_specs=[pl.BlockSpec((tm * nseg, V7X_LANES), lambda i: (i, 0)),
                   pl.BlockSpec((tm, V7X_LANES), lambda i: (i, 0)),
                   pl.BlockSpec((tm, V7X_LANES), lambda i: (i, 0))],
        out_shape=[jax.ShapeDtypeStruct((n * nseg, V7X_LANES), F32),
                   jax.ShapeDtypeStruct((n, V7X_LANES), jnp.int32),
                   jax.ShapeDtypeStruct((n, V7X_LANES), F32)],
        compiler_params=_cparams("parallel"),
        name="router",
    )(x, g.reshape(1, d), router_w, router_b.reshape(1, n_exp))


def _row_copy(src_hbm, row, dst, slot, sem):
    return pltpu.make_async_copy(src_hbm.at[pl.ds(row, 1), :], dst.at[pl.ds(slot, 1), :], sem)


def _token_pitch(d):
    return d // V7X_LANES + 1


def _token_copy(h_hbm, tok, buf, slot, sem, nseg, pitch):
    src = h_hbm.at[pl.ds(tok * pitch, nseg), :]
    dst = buf.at[pl.ds(slot * pitch, nseg), :]
    return pltpu.make_async_copy(src, dst, sem)


GATHER_UNROLL = 8


def _gather_kernel(nrows_ref, idx_ref, idx_next_ref, h_hbm, o_ref, buf, sem, *, tg, nseg, pitch):
    i = pl.program_id(0)

    def active(step):
        return jnp.logical_and(step < pl.num_programs(0), step * tg < nrows_ref[0])

    def issue(ids_ref, slot):
        def start(r8, carry):
            for u in range(GATHER_UNROLL):
                r = r8 * GATHER_UNROLL + u
                _token_copy(h_hbm, ids_ref[0, 0, r], buf.at[slot], r, sem.at[slot], nseg, pitch).start(priority=u % 2)
            return carry

        lax.fori_loop(0, tg // GATHER_UNROLL, start, 0)

    def drain(slot):
        def wait(r, carry):
            _token_copy(h_hbm, 0, buf.at[slot], r, sem.at[slot], nseg, pitch).wait()
            return carry

        lax.fori_loop(0, tg, wait, 0, unroll=GATHER_UNROLL)
        for s in range(nseg):
            seg = buf[slot, pl.ds(s, tg, stride=pitch), :]
            o_ref[:, s * V7X_LANES:(s + 1) * V7X_LANES] = seg.astype(o_ref.dtype)

    for slot in range(2):
        @pl.when(i % 2 == slot)
        def _():
            @pl.when(jnp.logical_and(i == 0, active(i)))
            def _():
                issue(idx_ref, slot)

            @pl.when(active(i + 1))
            def _():
                issue(idx_next_ref, 1 - slot)

            @pl.when(active(i))
            def _():
                drain(slot)


def _gather_rows(h_tok, src, n_rows, d, tg=256):
    m = src.shape[0]
    nseg = d // V7X_LANES
    pitch = _token_pitch(d)
    return pl.pallas_call(
        functools.partial(_gather_kernel, tg=tg, nseg=nseg, pitch=pitch),
        grid_spec=pltpu.PrefetchScalarGridSpec(
            num_scalar_prefetch=1,
            grid=(m // tg,),
            in_specs=[pl.BlockSpec((1, 1, tg), lambda i, nr: (i, 0, 0), memory_space=pltpu.SMEM),
                      pl.BlockSpec((1, 1, tg), lambda i, nr: (jnp.minimum(i + 1, m // tg - 1), 0, 0),
                                   memory_space=pltpu.SMEM),
                      pl.BlockSpec(memory_space=pl.ANY)],
            out_specs=pl.BlockSpec((tg, d), lambda i, nr: (i, 0)),
            scratch_shapes=[pltpu.VMEM((2, tg * pitch, V7X_LANES), F32), pltpu.SemaphoreType.DMA((2,))]),
        out_shape=jax.ShapeDtypeStruct((m, d), BF16),
        compiler_params=_cparams("arbitrary"),
        name="moe_gather",
    )(n_rows, src.reshape(m // tg, 1, tg), src.reshape(m // tg, 1, tg), h_tok)


MOE_SUB_ROWS = 128


def _expert_weights(be_ref, nxt_ref, cnt_ref, gid_ref, ng_ref, copies):
    j, i = pl.program_id(0), pl.program_id(1)
    first = jnp.logical_and(cnt_ref[i] > 0,
                            jnp.logical_or(i == 0, be_ref[i] != be_ref[jnp.maximum(i - 1, 0)]))
    slot = (j * ng_ref[0] + gid_ref[i]) % 2

    @pl.when(first)
    def _():
        @pl.when(jnp.logical_and(j == 0, i == 0))
        def _():
            for cp in copies(be_ref[0], 0, slot):
                cp.start()

        for cp in copies(be_ref[i], j, slot):
            cp.wait()
        nxt = nxt_ref[i]

        @pl.when(nxt >= 0)
        def _():
            for cp in copies(nxt, j, 1 - slot):
                cp.start()

        @pl.when(jnp.logical_and(nxt < 0, j + 1 < pl.num_programs(0)))
        def _():
            for cp in copies(be_ref[0], j + 1, 1 - slot):
                cp.start()

    return slot


def _ffn1_kernel(be_ref, nu_ref, nxt_ref, cnt_ref, gid_ref, ng_ref, x_ref, w1_hbm, bg_ref, bu_ref, o_ref,
                 wland, sem, *, ff, tf):
    def copies(e, jj, slot):
        return [pltpu.make_async_copy(w1_hbm.at[e, :, pl.ds(pl.multiple_of(half * ff + jj * tf, tf), tf)],
                                      wland.at[slot, half], sem.at[slot, half]) for half in range(2)]

    slot = _expert_weights(be_ref, nxt_ref, cnt_ref, gid_ref, ng_ref, copies)
    i = pl.program_id(1)
    for s in range(x_ref.shape[0] // MOE_SUB_ROWS):
        @pl.when(s * MOE_SUB_ROWS < cnt_ref[i])
        def _():
            rs = slice(s * MOE_SUB_ROWS, (s + 1) * MOE_SUB_ROWS)
            x = x_ref[rs, :]
            g_ = jnp.dot(x, wland[slot, 0], preferred_element_type=F32) + bg_ref[...]
            u_ = jnp.dot(x, wland[slot, 1], preferred_element_type=F32) + bu_ref[...]
            g_ = jnp.minimum(g_, SWIGLU_LIMIT)
            u_ = jnp.clip(u_, -SWIGLU_LIMIT, SWIGLU_LIMIT)
            act = (u_ + 1.0) * g_ * jax.nn.sigmoid(SWIGLU_ALPHA * g_)
            o_ref[rs, :] = act.astype(o_ref.dtype)


def _ffn2_kernel(be_ref, nu_ref, nxt_ref, cnt_ref, gid_ref, ng_ref, x_ref, w2_hbm, b_ref, o_ref,
                 wland, sem, *, tn):
    def copies(e, jj, slot):
        return [pltpu.make_async_copy(w2_hbm.at[e, :, pl.ds(pl.multiple_of(jj * tn, tn), tn)],
                                      wland.at[slot], sem.at[slot])]

    slot = _expert_weights(be_ref, nxt_ref, cnt_ref, gid_ref, ng_ref, copies)
    i = pl.program_id(1)
    for s in range(x_ref.shape[0] // MOE_SUB_ROWS):
        @pl.when(s * MOE_SUB_ROWS < cnt_ref[i])
        def _():
            rs = slice(s * MOE_SUB_ROWS, (s + 1) * MOE_SUB_ROWS)
            o_ref[rs, :] = jnp.dot(x_ref[rs, :], wland[slot], preferred_element_type=F32) + b_ref[...]


def _expert_ffn(rows, layout, w1, b1, w2, b2, *, tm, tf=512, tn=1024):
    m_pad, d = rows.shape
    n_exp, _, ff2 = w1.shape
    ff = ff2 // 2
    n_blk = m_pad // tm
    tf = _tile(ff, tf, V7X_LANES)
    tn = _tile(d, tn, V7X_LANES)
    nf = ff // tf
    assert tm % MOE_SUB_ROWS == 0

    def row_blk(i, pf):
        return jnp.minimum(i, pf[1][0] - 1)

    def expert(i, pf):
        return pf[0][i]

    act = pl.pallas_call(
        functools.partial(_ffn1_kernel, ff=ff, tf=tf),
        grid_spec=pltpu.PrefetchScalarGridSpec(
            num_scalar_prefetch=len(layout),
            grid=(nf, n_blk),
            in_specs=[pl.BlockSpec((tm, d), lambda j, i, *pf: (row_blk(i, pf), 0)),
                      pl.BlockSpec(memory_space=pl.ANY),
                      pl.BlockSpec((None, 1, tf), lambda j, i, *pf: (expert(i, pf), 0, j)),
                      pl.BlockSpec((None, 1, tf), lambda j, i, *pf: (expert(i, pf), 0, j + nf))],
            out_specs=pl.BlockSpec((tm, tf), lambda j, i, *pf: (row_blk(i, pf), j)),
            scratch_shapes=[pltpu.VMEM((2, 2, d, tf), F32), pltpu.SemaphoreType.DMA((2, 2))]),
        out_shape=jax.ShapeDtypeStruct((m_pad, ff), BF16),
        compiler_params=_cparams("arbitrary", "arbitrary"),
        name="moe_ffn1",
    )(*layout, rows, w1, b1.reshape(n_exp, 1, ff2), b1.reshape(n_exp, 1, ff2))

    return pl.pallas_call(
        functools.partial(_ffn2_kernel, tn=tn),
        grid_spec=pltpu.PrefetchScalarGridSpec(
            num_scalar_prefetch=len(layout),
            grid=(d // tn, n_blk),
            in_specs=[pl.BlockSpec((tm, ff), lambda j, i, *pf: (row_blk(i, pf), 0)),
                      pl.BlockSpec(memory_space=pl.ANY),
                      pl.BlockSpec((None, 1, tn), lambda j, i, *pf: (expert(i, pf), 0, j))],
            out_specs=pl.BlockSpec((tm, tn), lambda j, i, *pf: (row_blk(i, pf), j)),
            scratch_shapes=[pltpu.VMEM((2, ff, tn), F32), pltpu.SemaphoreType.DMA((2,))]),
        out_shape=jax.ShapeDtypeStruct((m_pad, d), F32),
        compiler_params=_cparams("arbitrary", "arbitrary"),
        name="moe_ffn2",
    )(*layout, act, w2, b2.reshape(n_exp, 1, d))


def _combine_kernel(dest_ref, ys_hbm, gate_ref, x_ref, g_ref, o_ref, buf, sem, *, tc, top_k):
    def start(r, carry):
        for j in range(top_k):
            _row_copy(ys_hbm, dest_ref[0, 0, r * top_k + j], buf.at[j], r, sem).start()
        return carry

    lax.fori_loop(0, tc, start, 0)

    def wait(r, carry):
        for j in range(top_k):
            _row_copy(ys_hbm, 0, buf.at[j], r, sem).wait()
        return carry

    lax.fori_loop(0, tc, wait, 0)
    gate = gate_ref[...]
    y = buf[0] * gate[:, 0:1]
    for j in range(1, top_k):
        y = y + buf[j] * gate[:, j:j + 1]
    x = x_ref[...] + y
    ms = jnp.mean(x * x, axis=-1, keepdims=True)
    o_ref[...] = x * lax.rsqrt(ms + RMS_EPS) * g_ref[...]


def _combine(ys, dest, gate, x, g, tc=128):
    n, d = x.shape
    tc = _tile(n, tc, V7X_SUBLANES)
    return pl.pallas_call(
        functools.partial(_combine_kernel, tc=tc, top_k=TOP_K),
        grid=(n // tc,),
        in_specs=[pl.BlockSpec((1, 1, tc * TOP_K), lambda i: (i, 0, 0), memory_space=pltpu.SMEM),
                  pl.BlockSpec(memory_space=pl.ANY),
                  pl.BlockSpec((tc, V7X_LANES), lambda i: (i, 0)),
                  pl.BlockSpec((tc, d), lambda i: (i, 0)),
                  pl.BlockSpec((1, d), lambda i: (0, 0))],
        out_specs=pl.BlockSpec((tc, d), lambda i: (i, 0)),
        out_shape=jax.ShapeDtypeStruct((n, d), F32),
        scratch_shapes=[pltpu.VMEM((TOP_K, tc, d), F32), pltpu.SemaphoreType.DMA(())],
        compiler_params=_cparams("arbitrary"),
        name="moe_combine",
    )(dest.reshape(n // tc, 1, tc * TOP_K), ys, gate, x, g.reshape(1, d))


def _moe_layout(top_i, n_exp, tm):
    n_tok = top_i.shape[0]
    m = n_tok * TOP_K
    e_flat = top_i.reshape(-1)
    onehot = (e_flat[:, None] == jnp.arange(n_exp, dtype=jnp.int32)[None, :]).astype(jnp.int32)
    rank = jnp.sum((jnp.cumsum(onehot, axis=0) - onehot) * onehot, axis=1)
    counts = jnp.sum(onehot, axis=0)
    pad_counts = (counts + tm - 1) // tm * tm
    pad_end = jnp.cumsum(pad_counts)
    pad_start = pad_end - pad_counts
    dest = pad_start[e_flat] + rank
    n_blk = (m + n_exp * (tm - 1)) // tm + 1
    tok = jnp.arange(m, dtype=jnp.int32) // TOP_K
    src = jnp.zeros((n_blk * tm,), jnp.int32).at[dest].set(tok)
    blk_start = jnp.arange(n_blk, dtype=jnp.int32) * tm
    blk_e = jnp.minimum(jnp.sum(pad_end[None, :] <= blk_start[:, None], axis=1), n_exp - 1).astype(jnp.int32)
    n_used = (pad_end[-1] // tm).astype(jnp.int32)
    blk = jnp.arange(n_blk, dtype=jnp.int32)
    used = blk < n_used
    cnt = jnp.where(used, jnp.clip(counts[blk_e] - (blk_start - pad_start[blk_e]), 0, tm), 0).astype(jnp.int32)
    starts = jnp.logical_and(used, jnp.concatenate([jnp.ones((1,), bool), blk_e[1:] != blk_e[:-1]]))
    start_idx = jnp.where(starts, blk, n_blk)
    nxt_idx = jnp.concatenate([lax.cummin(start_idx, reverse=True)[1:], jnp.full((1,), n_blk, jnp.int32)])
    nxt_e = jnp.where(nxt_idx < n_blk, blk_e[jnp.minimum(nxt_idx, n_blk - 1)], -1).astype(jnp.int32)
    gid = (jnp.cumsum(starts.astype(jnp.int32)) - 1).astype(jnp.int32)
    n_groups = jnp.sum(starts.astype(jnp.int32)).astype(jnp.int32).reshape(1)
    return dest.astype(jnp.int32), src, (blk_e, n_used.reshape(1), nxt_e, cnt, gid, n_groups)


def _pad_seq(x, t_pad):
    b, t, w = x.shape
    return x if t == t_pad else jnp.pad(x, ((0, 0), (0, t_pad - t), (0, 0)))


def kernel(x_prompt, x_sample, mem_prompt, state_gla, state_ssd, state_conv, cache_mem_k, cache_mem_v,
           norm_mix, w_in, gla_gate_w, gla_gate_b, gla_norm, ssd_conv_w, ssd_conv_b, ssd_dt_bias,
           ssd_a_log, ssd_d, ssd_norm, w_proj_gla, w_proj_ssd, w_out, norm_cross, norm_mem, w_cq,
           w_ck, w_cv, w_co, norm_ffn, router_w, router_b, moe_w1, moe_b1, moe_w2, moe_b2, norm_final):
    assert w_in.shape[0] == 1, "single layer"
    bp, tp, d = x_prompt.shape
    bs, ts, _ = x_sample.shape
    n_p, n_s = bp * tp, bs * ts
    n_all = n_p + n_s

    dv = gla_norm.shape[-1]
    val_w = w_proj_gla.shape[1]
    g_heads = val_w // dv
    key_w = gla_gate_w.shape[-1]
    rank = gla_gate_w.shape[1]
    inner = ssd_norm.shape[-1]
    s_heads = ssd_a_log.shape[-1]
    conv_dim = ssd_conv_w.shape[-1]
    n_state = state_ssd.shape[-1]
    s_groups = (conv_dim - inner) // (2 * n_state)
    kw = ssd_conv_w.shape[1]
    mem_heads, mem_hd = cache_mem_k.shape[-2:]
    mem_w = mem_heads * mem_hd
    n_exp = router_w.shape[-1]
    widths = (key_w, key_w, val_w, val_w, rank, inner, conv_dim, s_heads, d, d)
    offs = [0]
    for w_ in widths:
        offs.append(offs[-1] + w_)
    o_q, o_k, o_v, o_r, o_lr, o_z, o_xbc, o_dt, o_ga, o_gb, o_end = offs
    assert key_w == val_w // 2 and inner == d and val_w == d

    w_in_t = w_in.reshape(w_in.shape[1:]).T
    x_all = jnp.concatenate([x_prompt.reshape(n_p, d), x_sample.reshape(n_s, d)], axis=0)

    h = _rmsnorm(x_all, norm_mix[0], BF16)
    proj_a = _matmul_t(h, w_in_t, row0=0, nrows=o_lr)
    proj_b = _matmul_t(h, w_in_t, row0=o_z, nrows=o_dt - o_z)
    proj_c = _matmul_t(h, w_in_t, row0=o_ga, nrows=o_end - o_ga)
    small_wt = jnp.concatenate(
        [w_in_t[o_dt:o_ga], w_in_t[o_lr:o_z],
         jnp.zeros((V7X_LANES - s_heads - rank, d), F32)], axis=0)
    proj_s = _matmul_t(h, small_wt, row0=0, nrows=V7X_LANES)
    dt_off, lr_off = 0, s_heads

    gate_w, gate_b = gla_gate_w[0], gla_gate_b[0]
    t_pad = -(-ts // V7X_SUBLANES) * V7X_SUBLANES

    def sample_view(arr):
        return _pad_seq(arr[n_p:].reshape(bs, ts, arr.shape[1]), t_pad)

    kb = key_w
    gla_src_p = {"q": (proj_a, kb, 0), "k": (proj_a, kb, 1), "v": (proj_a, val_w, o_v // val_w),
                 "r": (proj_a, val_w, o_r // val_w), "lr": (proj_s, V7X_LANES, 0)}
    c_p = math.gcd(tp, GLA_CHUNK)
    o_p, gla_p = _gla_call(gla_src_p, gate_w, gate_b, gla_norm[0], None, heads=g_heads, batch=bp,
                           n_chunks=tp // c_p, c=c_p, t_valid=c_p, lr_off=lr_off, out_rows=n_all)
    pa_s, ps_s = sample_view(proj_a), sample_view(proj_s)
    gla_src_s = {"q": (pa_s, kb, 0), "k": (pa_s, kb, 1), "v": (pa_s, val_w, o_v // val_w),
                 "r": (pa_s, val_w, o_r // val_w), "lr": (ps_s, V7X_LANES, 0)}
    o_s, gla_s = _gla_call(gla_src_s, gate_w, gate_b, gla_norm[0], state_gla[0], heads=g_heads, batch=bs,
                           n_chunks=1, c=t_pad, t_valid=ts, lr_off=lr_off, out_rows=None)
    o_all = lax.dynamic_update_slice(o_p, o_s[:, :ts].reshape(n_s, val_w), (n_p, 0))

    bc_w = conv_dim - inner
    ssd_src_p = {"x": (proj_b, inner, 1), "bc": (proj_b, bc_w, (2 * inner) // bc_w),
                 "z": (proj_b, inner, 0), "dt": (proj_s, V7X_LANES, 0)}
    cs_p = math.gcd(tp, SSD_CHUNK)
    y_p, ssd_p, xbc_p = _ssd_call(
        ssd_src_p, ssd_conv_w[0], ssd_conv_b[0], ssd_dt_bias[0], ssd_a_log[0], ssd_d[0],
        ssd_norm[0], None, None, heads=s_heads, groups=s_groups, n=n_state, batch=bp,
        n_chunks=tp // cs_p, c=cs_p, t_valid=cs_p, dt_off=dt_off, out_rows=n_all)
    pb_s = sample_view(proj_b)
    ssd_src_s = {"x": (pb_s, inner, 1), "bc": (pb_s, bc_w, (2 * inner) // bc_w),
                 "z": (pb_s, inner, 0), "dt": (ps_s, V7X_LANES, 0)}
    y_s, ssd_s, xbc_s = _ssd_call(
        ssd_src_s, ssd_conv_w[0], ssd_conv_b[0], ssd_dt_bias[0], ssd_a_log[0], ssd_d[0],
        ssd_norm[0], state_conv[0], state_ssd[0], heads=s_heads, groups=s_groups,
        n=n_state, batch=bs, n_chunks=1, c=t_pad, t_valid=ts, dt_off=dt_off, out_rows=None)
    y_all = lax.dynamic_update_slice(y_p, y_s[:, :ts].reshape(n_s, inner), (n_p, 0))

    merged = _merge(o_all, y_all, w_proj_gla[0], w_proj_ssd[0], proj_c)
    x1 = _matmul(merged, w_out[0], res=x_all)

    h2 = _rmsnorm(x1, norm_cross[0], BF16)
    q_all = _matmul(h2, w_cq[0])
    mem = _rmsnorm(mem_prompt.reshape(-1, d), norm_mem[0], BF16)
    mk = _matmul(mem, w_ck[0])
    mv = _matmul(mem, w_cv[0])
    m_len = mem_prompt.shape[1]
    a_p = _attention(q_all[:n_p].reshape(bp, tp, mem_w), mk.reshape(bp, m_len, mem_w),
                     mv.reshape(bp, m_len, mem_w), heads=mem_heads)
    q_s = _pad_seq(q_all[n_p:].reshape(bs, ts, mem_w), t_pad)
    a_s = _attention(q_s, cache_mem_k[0].reshape(bs, -1, mem_w), cache_mem_v[0].reshape(bs, -1, mem_w),
                     heads=mem_heads)
    a_all = jnp.concatenate([a_p.reshape(n_p, mem_w), a_s[:, :ts].reshape(n_s, mem_w)], axis=0)
    x2 = _matmul(a_all, w_co[0], res=x1)

    tm_e = 512
    h3, top_pad, gate_pad = _router(x2, norm_ffn[0], router_w[0], router_b[0])
    dest, src, layout = _moe_layout(top_pad[:, :TOP_K], n_exp, tm_e)
    rows = _gather_rows(h3, src, layout[1] * tm_e, d)
    ys = _expert_ffn(rows, layout, moe_w1[0], moe_b1[0], moe_w2[0], moe_b2[0], tm=tm_e)
    y_fin = _combine(ys, dest, gate_pad, x2, norm_final)

    y_prompt = y_fin[:n_p].reshape(bp, tp, d)
    y_sample = y_fin[n_p:].reshape(bs, ts, d)
    return (y_prompt, y_sample, gla_p[None], ssd_p[None], xbc_p[None],
            mk.reshape(1, bp, m_len, mem_heads, mem_hd), mv.reshape(1, bp, m_len, mem_heads, mem_hd),
            gla_s[None], ssd_s[None], xbc_s[None])
```

```python
import functools
import math

import jax
import jax.numpy as jnp
from jax import lax
from jax.experimental import pallas as pl
from jax.experimental.pallas import tpu as pltpu

F32 = jnp.float32
BF16 = jnp.bfloat16

RMS_EPS = 1e-6
GLA_GATE_TAU = 16.0
GLA_CHUNK = 64
SSD_CHUNK = 64
TOP_K = 4
SWIGLU_LIMIT = 7.0
SWIGLU_ALPHA = 1.702

V7X_LANES = 128
V7X_SUBLANES = 8
V7X_VMEM_BYTES = 64 * 1024 * 1024
VMEM_LIMIT = V7X_VMEM_BYTES - 8 * 1024 * 1024

NT_DIMS = (((1,), (1,)), ((), ()))
TN_DIMS = (((0,), (0,)), ((), ()))


def _cparams(*sem):
    return pltpu.CompilerParams(dimension_semantics=sem, vmem_limit_bytes=VMEM_LIMIT)


def _tile(n, pref, quantum):
    t = (min(pref, n) // quantum) * quantum
    while t >= quantum:
        if n % t == 0:
            return t
        t -= quantum
    return n


def _dot(a, b):
    return jnp.dot(a.astype(BF16), b.astype(BF16), preferred_element_type=F32)


def _dot_nt(a, b):
    return lax.dot_general(a.astype(BF16), b.astype(BF16), NT_DIMS, preferred_element_type=F32)


def _dot_tn(a, b):
    return lax.dot_general(a.astype(BF16), b.astype(BF16), TN_DIMS, preferred_element_type=F32)


def _split3(x):
    hi = x.astype(BF16)
    r1 = x - hi.astype(F32)
    mid = r1.astype(BF16)
    lo = (r1 - mid.astype(F32)).astype(BF16)
    return hi, mid, lo


def _cumsum_rows(x, tri_bf):
    hi, mid, lo = _split3(x)
    n = x.shape[1]
    cat = jnp.concatenate([hi, mid, lo], axis=1)
    s = jnp.dot(tri_bf, cat, preferred_element_type=F32)
    return s[:, :n] + s[:, n:2 * n] + s[:, 2 * n:]


def _softplus(x):
    return jnp.maximum(x, 0.0) + jnp.log1p(jnp.exp(-jnp.abs(x)))


def _silu(x):
    return x * jax.nn.sigmoid(x)


def _tri_masks(c):
    row = lax.broadcasted_iota(jnp.int32, (c, c), 0)
    col = lax.broadcasted_iota(jnp.int32, (c, c), 1)
    mask = row >= col
    return mask, jnp.where(mask, 1.0, 0.0).astype(BF16)


def _rms_kernel(x_ref, g_ref, o_ref):
    x = x_ref[...]
    ms = jnp.mean(x * x, axis=-1, keepdims=True)
    o_ref[...] = (x * lax.rsqrt(ms + RMS_EPS) * g_ref[...]).astype(o_ref.dtype)


def _rmsnorm(x, g, out_dtype):
    n, d = x.shape
    tm = _tile(n, 256, V7X_SUBLANES)
    return pl.pallas_call(
        _rms_kernel,
        grid=(n // tm,),
        in_specs=[pl.BlockSpec((tm, d), lambda i: (i, 0)),
                  pl.BlockSpec((1, d), lambda i: (0, 0))],
        out_specs=pl.BlockSpec((tm, d), lambda i: (i, 0)),
        out_shape=jax.ShapeDtypeStruct((n, d), out_dtype),
        compiler_params=_cparams("parallel"),
        name="rmsnorm",
    )(x, g.reshape(1, d))


def _mm_kernel(*refs, has_res):
    if has_res:
        x_ref, w_ref, r_ref, o_ref = refs
    else:
        x_ref, w_ref, o_ref = refs
        r_ref = None
    acc = jnp.dot(x_ref[...], w_ref[...], preferred_element_type=F32)
    if r_ref is not None:
        acc = acc + r_ref[...]
    o_ref[...] = acc.astype(o_ref.dtype)


def _matmul(x, w, *, col0=0, ncols=None, res=None, out_dtype=F32, tm=512, tn=512):
    n, k = x.shape
    ncols = w.shape[1] - col0 if ncols is None else ncols
    tm = _tile(n, tm, V7X_SUBLANES)
    tn = _tile(math.gcd(ncols, col0) if col0 else ncols, tn, V7X_LANES)
    assert ncols % tn == 0 and col0 % tn == 0
    jb = col0 // tn
    in_specs = [pl.BlockSpec((tm, k), lambda j, i: (i, 0)),
                pl.BlockSpec((k, tn), lambda j, i: (0, j + jb))]
    args = [x, w]
    if res is not None:
        in_specs.append(pl.BlockSpec((tm, tn), lambda j, i: (i, j)))
        args.append(res)
    return pl.pallas_call(
        functools.partial(_mm_kernel, has_res=res is not None),
        grid=(ncols // tn, n // tm),
        in_specs=in_specs,
        out_specs=pl.BlockSpec((tm, tn), lambda j, i: (i, j)),
        out_shape=jax.ShapeDtypeStruct((n, ncols), out_dtype),
        compiler_params=_cparams("arbitrary", "arbitrary"),
        name="matmul",
    )(*args)


def _mmt_kernel(x_ref, wt_hbm, o_ref, wland, sem, *, row0, tn):
    j = pl.program_id(0)

    def tile_copy(jj):
        rows = pl.ds(pl.multiple_of(row0 + jj * tn, V7X_SUBLANES), tn)
        return pltpu.make_async_copy(wt_hbm.at[rows, :], wland.at[jj % 2], sem.at[jj % 2])

    @pl.when(pl.program_id(1) == 0)
    def _():
        @pl.when(j == 0)
        def _():
            tile_copy(0).start()

        tile_copy(j).wait()

        @pl.when(j + 1 < pl.num_programs(0))
        def _():
            tile_copy(j + 1).start()

    o_ref[...] = lax.dot_general(x_ref[...], wland[j % 2], NT_DIMS, preferred_element_type=F32)


def _matmul_t(x, wt, *, row0, nrows, tm=512, tn=1024):
    n, k = x.shape
    tm = _tile(n, tm, V7X_SUBLANES)
    tn = _tile(nrows, tn, V7X_LANES)
    assert row0 % V7X_SUBLANES == 0 and nrows % tn == 0
    return pl.pallas_call(
        functools.partial(_mmt_kernel, row0=row0, tn=tn),
        grid=(nrows // tn, n // tm),
        in_specs=[pl.BlockSpec((tm, k), lambda j, i: (i, 0)),
                  pl.BlockSpec(memory_space=pl.ANY)],
        out_specs=pl.BlockSpec((tm, tn), lambda j, i: (i, j)),
        out_shape=jax.ShapeDtypeStruct((n, nrows), F32),
        scratch_shapes=[pltpu.VMEM((2, tn, k), F32), pltpu.SemaphoreType.DMA((2,))],
        compiler_params=_cparams("arbitrary", "arbitrary"),
        name="matmul_t",
    )(x, wt)


def _merge_kernel(o_ref, y_ref, wg_hbm, ws_hbm, ga_ref, gb_ref, out_ref, wland, wgb_ref, wsb_ref, sem, *, tn):
    j = pl.program_id(0)

    def tile_copies(jj):
        cols = pl.ds(pl.multiple_of(jj * tn, tn), tn)
        return [pltpu.make_async_copy(w.at[:, cols], wland.at[s], sem.at[s])
                for s, w in enumerate((wg_hbm, ws_hbm))]

    @pl.when(pl.program_id(1) == 0)
    def _():
        @pl.when(j == 0)
        def _():
            for cp in tile_copies(0):
                cp.start()

        for cp in tile_copies(j):
            cp.wait()
        wgb_ref[...] = wland[0].astype(BF16)
        wsb_ref[...] = wland[1].astype(BF16)

        @pl.when(j + 1 < pl.num_programs(0))
        def _():
            for cp in tile_copies(j + 1):
                cp.start()

    pg = jnp.dot(o_ref[...], wgb_ref[...], preferred_element_type=F32)
    ps = jnp.dot(y_ref[...], wsb_ref[...], preferred_element_type=F32)
    out = jax.nn.sigmoid(ga_ref[...]) * pg + jax.nn.sigmoid(gb_ref[...]) * ps
    out_ref[...] = out.astype(out_ref.dtype)


def _merge(o_gla, y_ssd, w_pg, w_ps, gates, tm=256, tn=512):
    n, k = o_gla.shape
    d = w_pg.shape[1]
    tm = _tile(n, tm, V7X_SUBLANES)
    tn = _tile(d, tn, V7X_LANES)
    nj = d // tn
    return pl.pallas_call(
        functools.partial(_merge_kernel, tn=tn),
        grid=(nj, n // tm),
        in_specs=[pl.BlockSpec((tm, k), lambda j, i: (i, 0)),
                  pl.BlockSpec((tm, k), lambda j, i: (i, 0)),
                  pl.BlockSpec(memory_space=pl.ANY),
                  pl.BlockSpec(memory_space=pl.ANY),
                  pl.BlockSpec((tm, tn), lambda j, i: (i, j)),
                  pl.BlockSpec((tm, tn), lambda j, i: (i, j + nj))],
        out_specs=pl.BlockSpec((tm, tn), lambda j, i: (i, j)),
        out_shape=jax.ShapeDtypeStruct((n, d), BF16),
        scratch_shapes=[pltpu.VMEM((2, k, tn), F32), pltpu.VMEM((k, tn), BF16), pltpu.VMEM((k, tn), BF16),
                        pltpu.SemaphoreType.DMA((2,))],
        compiler_params=_cparams("arbitrary", "arbitrary"),
        name="merge",
    )(o_gla, y_ssd, w_pg, w_ps, gates, gates)


def _gla_kernel(*refs, heads, dk, dv, c, t_valid, lr_off, rank, has_s0):
    if has_s0:
        q_ref, k_ref, v_ref, r_ref, lr_ref, gw_ref, gb_ref, gn_ref, s0_ref, o_ref, so_ref, s_scr = refs
    else:
        q_ref, k_ref, v_ref, r_ref, lr_ref, gw_ref, gb_ref, gn_ref, o_ref, so_ref, s_scr = refs
        s0_ref = None
    ci = pl.program_id(1)

    @pl.when(ci == 0)
    def _():
        if s0_ref is None:
            s_scr[...] = jnp.zeros_like(s_scr)
        else:
            s_scr[...] = s0_ref[...]

    mask, tri_bf = _tri_masks(c)
    kw = heads * dk
    scale = dk ** -0.5
    gn = gn_ref[...]
    lr = lr_ref[:, lr_off:lr_off + rank]
    zg = jnp.dot(lr, gw_ref[...], preferred_element_type=F32) + gb_ref[...]
    la = (jnp.minimum(zg, 0.0) - jnp.log1p(jnp.exp(-jnp.abs(zg)))) * (1.0 / GLA_GATE_TAU)
    if t_valid < c:
        la = jnp.where(lax.broadcasted_iota(jnp.int32, (c, kw), 0) < t_valid, la, 0.0)
    b = _cumsum_rows(la, tri_bf)
    k_all = k_ref[...]
    qe = ((q_ref[...] * scale) * jnp.exp(b)).astype(BF16)
    ke = (k_all * jnp.exp(-b)).astype(BF16)
    b_last = b[c - 1:c, :]
    kd = (k_all * jnp.exp(b_last - b)).astype(BF16)
    e_hi, e_mid, e_lo = (t.astype(F32) for t in _split3(jnp.exp(b_last)))
    row8 = lax.broadcasted_iota(jnp.int32, (V7X_SUBLANES, kw), 0)
    e3 = jnp.where(row8 == 0, e_hi, jnp.where(row8 == 1, e_mid, jnp.where(row8 == 2, e_lo, 0.0)))
    ones8 = jnp.ones((V7X_SUBLANES, dv), F32)
    atts = []
    for h in range(heads):
        ks = slice(h * dk, (h + 1) * dk)
        att = lax.dot_general(qe[:, ks], ke[:, ks], NT_DIMS, preferred_element_type=F32)
        atts.append(jnp.where(mask, att, 0.0).astype(BF16))
    for h in range(heads):
        ks = slice(h * dk, (h + 1) * dk)
        vs = slice(h * dv, (h + 1) * dv)
        v = v_ref[:, vs].astype(BF16)
        s_old = s_scr[h]
        o = (jnp.dot(atts[h], v, preferred_element_type=F32)
             + jnp.dot(qe[:, ks], s_old.astype(BF16), preferred_element_type=F32))
        e_full = lax.dot_general(e3[:, ks], ones8, TN_DIMS, preferred_element_type=F32)
        s_scr[h] = e_full * s_old + lax.dot_general(kd[:, ks], v, TN_DIMS, preferred_element_type=F32)
        ms = jnp.mean(o * o, axis=-1, keepdims=True)
        og = o * lax.rsqrt(ms + RMS_EPS) * gn * _silu(r_ref[:, vs])
        o_ref[:, vs] = og.astype(o_ref.dtype)

    @pl.when(ci == pl.num_programs(1) - 1)
    def _():
        so_ref[...] = s_scr[...]


def _gla_call(srcs, gate_w, gate_b, gnorm, s0, *, heads, batch, n_chunks, c, t_valid, lr_off, out_rows):
    rank, kw = gate_w.shape
    dk = kw // heads
    dv = gnorm.shape[-1]
    vw = heads * dv

    def spec(name):
        arr, width, cb = srcs[name]
        if arr.ndim == 3:
            return pl.BlockSpec((None, c, width), lambda b, ci: (b, ci, cb))
        return pl.BlockSpec((c, width), lambda b, ci: (b * n_chunks + ci, cb))

    names = ("q", "k", "v", "r", "lr")
    in_specs = [spec(nm) for nm in names] + [
        pl.BlockSpec((rank, kw), lambda b, ci: (0, 0)),
        pl.BlockSpec((1, kw), lambda b, ci: (0, 0)),
        pl.BlockSpec((1, dv), lambda b, ci: (0, 0)),
    ]
    args = [srcs[nm][0] for nm in names] + [gate_w, gate_b.reshape(1, kw), gnorm.reshape(1, dv)]
    if s0 is not None:
        in_specs.append(pl.BlockSpec((None, heads, dk, dv), lambda b, ci: (b, 0, 0, 0)))
        args.append(s0)
    three_d = srcs["q"][0].ndim == 3
    if three_d:
        o_spec = pl.BlockSpec((None, c, vw), lambda b, ci: (b, ci, 0))
        o_shape = jax.ShapeDtypeStruct((batch, n_chunks * c, vw), BF16)
    else:
        o_spec = pl.BlockSpec((c, vw), lambda b, ci: (b * n_chunks + ci, 0))
        o_shape = jax.ShapeDtypeStruct((out_rows, vw), BF16)
    return pl.pallas_call(
        functools.partial(_gla_kernel, heads=heads, dk=dk, dv=dv, c=c, t_valid=t_valid,
                          lr_off=lr_off, rank=rank, has_s0=s0 is not None),
        grid=(batch, n_chunks),
        in_specs=in_specs,
        out_specs=[o_spec, pl.BlockSpec((None, heads, dk, dv), lambda b, ci: (b, 0, 0, 0))],
        out_shape=[o_shape, jax.ShapeDtypeStruct((batch, heads, dk, dv), F32)],
        scratch_shapes=[pltpu.VMEM((heads, dk, dv), F32)],
        compiler_params=_cparams("parallel", "arbitrary"),
        name="gla",
    )(*args)


def _ssd_kernel(*refs, heads, groups, p, n, c, t_valid, dt_off, conv_w, has_state):
    if has_state:
        (x_ref, bc_ref, z_ref, dt_ref, cw_ref, cb_ref, dtb_ref, alog_ref, dexp_ref, nrm_ref,
         cs_ref, h0_ref, y_ref, ho_ref, cso_ref, h_scr, xf_scr, bcf_scr, y_scr) = refs
    else:
        (x_ref, bc_ref, z_ref, dt_ref, cw_ref, cb_ref, dtb_ref, alog_ref, dexp_ref, nrm_ref,
         y_ref, ho_ref, cso_ref, h_scr, xf_scr, bcf_scr, y_scr) = refs
        cs_ref = h0_ref = None
    ci = pl.program_id(1)
    inner = heads * p
    gn = groups * n
    r_per_g = heads // groups
    pad = V7X_SUBLANES
    hist = conv_w - 1

    @pl.when(ci == 0)
    def _():
        if has_state:
            h_scr[...] = h0_ref[...]
            xf_scr[0:pad, :] = jnp.zeros((pad, inner), F32)
            bcf_scr[0:pad, :] = jnp.zeros((pad, 2 * gn), F32)
            xf_scr[pad - hist:pad, :] = cs_ref[:, 0:inner]
            bcf_scr[pad - hist:pad, :] = cs_ref[:, inner:inner + 2 * gn]
        else:
            h_scr[...] = jnp.zeros_like(h_scr)
            xf_scr[0:pad, :] = jnp.zeros((pad, inner), F32)
            bcf_scr[0:pad, :] = jnp.zeros((pad, 2 * gn), F32)

    xf_scr[pad:pad + c, :] = x_ref[...]
    bcf_scr[pad:pad + c, :] = bc_ref[...]
    xc = cb_ref[:, 0:inner] + xf_scr[pad - hist:pad - hist + c, :] * cw_ref[0:1, 0:inner]
    bcc = cb_ref[:, inner:] + bcf_scr[pad - hist:pad - hist + c, :] * cw_ref[0:1, inner:]
    for j in range(1, conv_w):
        xc = xc + xf_scr[pad - hist + j:pad - hist + j + c, :] * cw_ref[j:j + 1, 0:inner]
        bcc = bcc + bcf_scr[pad - hist + j:pad - hist + j + c, :] * cw_ref[j:j + 1, inner:]
    xs = _silu(xc)
    bcs = _silu(bcc)
    xf_scr[0:pad, :] = x_ref[c - pad:c, :]
    bcf_scr[0:pad, :] = bc_ref[c - pad:c, :]

    dt = _softplus(dt_ref[:, dt_off:dt_off + heads] + dtb_ref[...])
    if t_valid < c:
        dt = jnp.where(lax.broadcasted_iota(jnp.int32, (c, heads), 0) < t_valid, dt, 0.0)
    a_neg = -jnp.exp(alog_ref[...])
    mask, tri_bf = _tri_masks(c)
    cum = _cumsum_rows(dt * a_neg, tri_bf)
    sq = V7X_LANES
    both = jnp.concatenate([cum, dt], axis=1)
    if both.shape[1] < sq:
        both = jnp.concatenate([both, jnp.zeros((c, sq - both.shape[1]), F32)], axis=1)
    if c < sq:
        both = jnp.concatenate([both, jnp.zeros((sq - c, sq), F32)], axis=0)
    both_t = both.T
    e_cum = jnp.exp(cum)
    last = cum[c - 1:c, :]
    wk = jnp.exp(last - cum) * dt
    e_last = jnp.exp(last)

    paired = c == p and 2 * p == V7X_LANES and r_per_g % 2 == 0
    if paired:
        lane2 = lax.broadcasted_iota(jnp.int32, (c, 2 * p), 1)
        lo = lane2 < p
        mask2 = lax.broadcasted_iota(jnp.int32, (c, 2 * p), 0) >= jnp.where(lo, lane2, lane2 - p)
    for g in range(groups):
        bg = bcs[:, g * n:(g + 1) * n]
        cg = bcs[:, gn + g * n:gn + (g + 1) * n]
        cbm = _dot_nt(cg, bg)
        hg = h_scr[g]
        ych = _dot_nt(cg, hg)
        xw_parts = []
        if paired:
            cb2 = jnp.concatenate([cbm, cbm], axis=1)
            for pr in range(r_per_g // 2):
                ha = g * r_per_g + 2 * pr
                hb = ha + 1
                lanes = slice(ha * p, (ha + 2) * p)

                def both(arr, ha=ha, hb=hb):
                    return jnp.where(lo, arr[:, ha:ha + 1], arr[:, hb:hb + 1])

                def rows(r0, ha=ha, hb=hb):
                    return jnp.concatenate([both_t[r0 + ha:r0 + ha + 1, 0:c],
                                            both_t[r0 + hb:r0 + hb + 1, 0:c]], axis=1)

                decay = jnp.exp(jnp.where(mask2, both(cum) - rows(0), -jnp.inf))
                w = cb2 * decay * rows(heads)
                xp = xs[:, lanes]
                x2 = jnp.concatenate([jnp.where(lo, xp, 0.0), jnp.where(lo, 0.0, xp)], axis=0)
                y_scr[:, lanes] = _dot(w, x2) + ych[:, 2 * pr * p:(2 * pr + 2) * p] * both(e_cum)
                xw_parts.append(xp * both(wk))
        for r in range(0 if paired else r_per_g):
            hd = g * r_per_g + r
            cum_col = cum[:, hd:hd + 1]
            cum_row = both_t[hd:hd + 1, 0:c]
            dt_row = both_t[heads + hd:heads + hd + 1, 0:c]
            decay = jnp.exp(jnp.where(mask, cum_col - cum_row, -jnp.inf))
            w = cbm * decay * dt_row
            xh = xs[:, hd * p:(hd + 1) * p]
            yh = _dot(w, xh) + ych[:, r * p:(r + 1) * p] * e_cum[:, hd:hd + 1]
            y_scr[:, hd * p:(hd + 1) * p] = yh
            xw_parts.append(xh * wk[:, hd:hd + 1])
        xw = jnp.concatenate(xw_parts, axis=1)
        upd = _dot_tn(xw, bg)
        for r in range(r_per_g):
            hd = g * r_per_g + r
            rs = slice(r * p, (r + 1) * p)
            h_scr[g, rs, :] = e_last[:, hd:hd + 1] * hg[rs, :] + upd[rs, :]

    y = (y_scr[...] + dexp_ref[...] * xs) * _silu(z_ref[...])
    gw = inner // groups
    for g in range(groups):
        gs = slice(g * gw, (g + 1) * gw)
        yg = y[:, gs]
        ms = jnp.mean(yg * yg, axis=-1, keepdims=True)
        y_ref[:, gs] = (yg * lax.rsqrt(ms + RMS_EPS) * nrm_ref[:, gs]).astype(y_ref.dtype)

    @pl.when(ci == pl.num_programs(1) - 1)
    def _():
        ho_ref[...] = h_scr[...]
        cso_ref[:, 0:inner] = x_ref[t_valid - hist:t_valid, :]
        cso_ref[:, inner:] = bc_ref[t_valid - hist:t_valid, :]


def _ssd_call(srcs, conv_w, conv_b, dt_bias, a_log, d_skip, ssd_norm, conv_state, h0, *,
              heads, groups, n, batch, n_chunks, c, t_valid, dt_off, out_rows):
    kw, conv_dim = conv_w.shape
    inner = ssd_norm.shape[-1]
    p = inner // heads
    gn = groups * n
    r_per_g = heads // groups

    def spec(name):
        arr, width, cb = srcs[name]
        if arr.ndim == 3:
            return pl.BlockSpec((None, c, width), lambda b, ci: (b, ci, cb))
        return pl.BlockSpec((c, width), lambda b, ci: (b * n_chunks + ci, cb))

    def full(shape):
        return pl.BlockSpec(shape, lambda b, ci: (0,) * len(shape))

    names = ("x", "bc", "z", "dt")
    in_specs = [spec(nm) for nm in names] + [
        full((kw, conv_dim)), full((1, conv_dim)), full((1, heads)), full((1, heads)),
        full((1, inner)), full((1, inner))]
    args = [srcs[nm][0] for nm in names] + [
        conv_w, conv_b.reshape(1, conv_dim), dt_bias.reshape(1, heads), a_log.reshape(1, heads),
        jnp.repeat(d_skip, p).reshape(1, inner), ssd_norm.reshape(1, inner)]
    has_state = h0 is not None
    if has_state:
        in_specs += [pl.BlockSpec((None, kw - 1, conv_dim), lambda b, ci: (b, 0, 0)),
                     pl.BlockSpec((None, groups, r_per_g * p, n), lambda b, ci: (b, 0, 0, 0))]
        args += [conv_state, h0.reshape(batch, groups, r_per_g * p, n)]
    three_d = srcs["x"][0].ndim == 3
    if three_d:
        y_spec = pl.BlockSpec((None, c, inner), lambda b, ci: (b, ci, 0))
        y_shape = jax.ShapeDtypeStruct((batch, n_chunks * c, inner), BF16)
    else:
        y_spec = pl.BlockSpec((c, inner), lambda b, ci: (b * n_chunks + ci, 0))
        y_shape = jax.ShapeDtypeStruct((out_rows, inner), BF16)
    assert t_valid >= kw - 1, "the new conv state must lie inside the last chunk"
    y, h_new, conv_new = pl.pallas_call(
        functools.partial(_ssd_kernel, heads=heads, groups=groups, p=p, n=n, c=c, t_valid=t_valid,
                          dt_off=dt_off, conv_w=kw, has_state=has_state),
        grid=(batch, n_chunks),
        in_specs=in_specs,
        out_specs=[y_spec, pl.BlockSpec((None, groups, r_per_g * p, n), lambda b, ci: (b, 0, 0, 0)),
                   pl.BlockSpec((None, kw - 1, conv_dim), lambda b, ci: (b, 0, 0))],
        out_shape=[y_shape, jax.ShapeDtypeStruct((batch, groups, r_per_g * p, n), F32),
                   jax.ShapeDtypeStruct((batch, kw - 1, conv_dim), F32)],
        scratch_shapes=[pltpu.VMEM((groups, r_per_g * p, n), F32),
                        pltpu.VMEM((V7X_SUBLANES + c, inner), F32),
                        pltpu.VMEM((V7X_SUBLANES + c, 2 * gn), F32),
                        pltpu.VMEM((c, inner), F32)],
        compiler_params=_cparams("parallel", "arbitrary"),
        name="ssd",
    )(*args)
    return y, h_new.reshape(batch, heads, p, n), conv_new


def _attn_kernel(q_ref, k_ref, v_ref, o_ref, *, heads, hd):
    scale = hd ** -0.5
    for h in range(heads):
        hs = slice(h * hd, (h + 1) * hd)
        s = _dot_nt(q_ref[:, hs], k_ref[:, hs]) * scale
        m = jnp.max(s, axis=-1, keepdims=True)
        e = jnp.exp(s - m)
        prob = e / jnp.sum(e, axis=-1, keepdims=True)
        o_ref[:, hs] = _dot(prob, v_ref[:, hs]).astype(o_ref.dtype)


def _attention(q, mem_k, mem_v, *, heads, tq=512):
    batch, t, w = q.shape
    m = mem_k.shape[1]
    tq = _tile(t, tq, V7X_SUBLANES)
    return pl.pallas_call(
        functools.partial(_attn_kernel, heads=heads, hd=w // heads),
        grid=(batch, t // tq),
        in_specs=[pl.BlockSpec((None, tq, w), lambda b, i: (b, i, 0)),
                  pl.BlockSpec((None, m, w), lambda b, i: (b, 0, 0)),
                  pl.BlockSpec((None, m, w), lambda b, i: (b, 0, 0))],
        out_specs=pl.BlockSpec((None, tq, w), lambda b, i: (b, i, 0)),
        out_shape=jax.ShapeDtypeStruct((batch, t, w), BF16),
        compiler_params=_cparams("parallel", "arbitrary"),
        name="cross_attention",
    )(q, mem_k, mem_v)


def _router_kernel(x_ref, g_ref, rw_ref, rb_ref, h_ref, idx_ref, gate_ref, *, n_exp, top_k):
    x = x_ref[...]
    ms = jnp.mean(x * x, axis=-1, keepdims=True)
    h = x * lax.rsqrt(ms + RMS_EPS) * g_ref[...]
    tm, d = x.shape
    nseg = d // V7X_LANES
    pitch = _token_pitch(d)
    for s in range(nseg):
        h_ref[pl.ds(s, tm, stride=pitch), :] = h[:, s * V7X_LANES:(s + 1) * V7X_LANES]
    h_hi, h_mid, h_lo = _split3(h)
    w_hi, w_mid, w_lo = _split3(rw_ref[...])

    def d(a, b):
        return jnp.dot(a, b, preferred_element_type=F32)

    logits = (d(h_hi, w_hi) + (d(h_hi, w_mid) + d(h_mid, w_hi))
              + (d(h_hi, w_lo) + d(h_mid, w_mid) + d(h_lo, w_hi))) + rb_ref[...]
    lanes = lax.broadcasted_iota(jnp.int32, logits.shape, 1).astype(F32)
    work = logits
    vals, idxs = [], []
    for _ in range(top_k):
        mx = jnp.max(work, axis=-1, keepdims=True)
        ix = jnp.min(jnp.where(work == mx, lanes, float(n_exp)), axis=-1, keepdims=True)
        vals.append(mx)
        idxs.append(ix)
        work = jnp.where(lanes == ix, -jnp.inf, work)
    es = [jnp.exp(v - vals[0]) for v in vals]
    tot = es[0]
    for e in es[1:]:
        tot = tot + e
    out_lanes = lax.broadcasted_iota(jnp.int32, idx_ref.shape, 1)
    idx_out = jnp.zeros(idx_ref.shape, jnp.int32)
    gate_out = jnp.zeros(gate_ref.shape, F32)
    for j in range(top_k):
        idx_out = jnp.where(out_lanes == j, idxs[j].astype(jnp.int32), idx_out)
        gate_out = jnp.where(out_lanes == j, es[j] / tot, gate_out)
    idx_ref[...] = idx_out
    gate_ref[...] = gate_out


def _router(x, g, router_w, router_b):
    n, d = x.shape
    n_exp = router_w.shape[1]
    tm = _tile(n, 256, V7X_SUBLANES)
    nseg = _token_pitch(d)
    return pl.pallas_call(
        functools.partial(_router_kernel, n_exp=n_exp, top_k=TOP_K),
        grid=(n // tm,),
        in_specs=[pl.BlockSpec((tm, d), lambda i: (i, 0)),
                  pl.BlockSpec((1, d), lambda i: (0, 0)),
                  pl.BlockSpec((d, n_exp), lambda i: (0, 0)),
                  pl.BlockSpec((1, n_exp), lambda i: (0, 0))],
        out_specs=[pl.BlockSpec((tm * nseg, V7X_LANES), lambda i: (i, 0)),
                   pl.BlockSpec((tm, V7X_LANES), lambda i: (i, 0)),
                   pl.BlockSpec((tm, V7X_LANES), lambda i: (i, 0))],
        out_shape=[jax.ShapeDtypeStruct((n * nseg, V7X_LANES), F32),
                   jax.ShapeDtypeStruct((n, V7X_LANES), jnp.int32),
                   jax.ShapeDtypeStruct((n, V7X_LANES), F32)],
        compiler_params=_cparams("parallel"),
        name="router",
    )(x, g.reshape(1, d), router_w, router_b.reshape(1, n_exp))


def _row_copy(src_hbm, row, dst, slot, sem):
    return pltpu.make_async_copy(src_hbm.at[pl.ds(row, 1), :], dst.at[pl.ds(slot, 1), :], sem)


def _token_pitch(d):
    return d // V7X_LANES + 1


def _token_copy(h_hbm, tok, buf, slot, sem, nseg, pitch):
    src = h_hbm.at[pl.ds(tok * pitch, nseg), :]
    dst = buf.at[pl.ds(slot * pitch, nseg), :]
    return pltpu.make_async_copy(src, dst, sem)


GATHER_UNROLL = 8


def _gather_kernel(nrows_ref, idx_ref, idx_next_ref, h_hbm, o_ref, buf, sem, *, tg, nseg, pitch):
    i = pl.program_id(0)

    def active(step):
        return jnp.logical_and(step < pl.num_programs(0), step * tg < nrows_ref[0])

    def issue(ids_ref, slot):
        def start(r8, carry):
            for u in range(GATHER_UNROLL):
                r = r8 * GATHER_UNROLL + u
                _token_copy(h_hbm, ids_ref[0, 0, r], buf.at[slot], r, sem.at[slot], nseg, pitch).start(priority=u % 2)
            return carry

        lax.fori_loop(0, tg // GATHER_UNROLL, start, 0)

    def drain(slot):
        def wait(r, carry):
            _token_copy(h_hbm, 0, buf.at[slot], r, sem.at[slot], nseg, pitch).wait()
            return carry

        lax.fori_loop(0, tg, wait, 0, unroll=GATHER_UNROLL)
        for s in range(nseg):
            seg = buf[slot, pl.ds(s, tg, stride=pitch), :]
            o_ref[:, s * V7X_LANES:(s + 1) * V7X_LANES] = seg.astype(o_ref.dtype)

    for slot in range(2):
        @pl.when(i % 2 == slot)
        def _():
            @pl.when(jnp.logical_and(i == 0, active(i)))
            def _():
                issue(idx_ref, slot)

            @pl.when(active(i + 1))
            def _():
                issue(idx_next_ref, 1 - slot)

            @pl.when(active(i))
            def _():
                drain(slot)


def _gather_rows(h_tok, src, n_rows, d, tg=256):
    m = src.shape[0]
    nseg = d // V7X_LANES
    pitch = _token_pitch(d)
    return pl.pallas_call(
        functools.partial(_gather_kernel, tg=tg, nseg=nseg, pitch=pitch),
        grid_spec=pltpu.PrefetchScalarGridSpec(
            num_scalar_prefetch=1,
            grid=(m // tg,),
            in_specs=[pl.BlockSpec((1, 1, tg), lambda i, nr: (i, 0, 0), memory_space=pltpu.SMEM),
                      pl.BlockSpec((1, 1, tg), lambda i, nr: (jnp.minimum(i + 1, m // tg - 1), 0, 0),
                                   memory_space=pltpu.SMEM),
                      pl.BlockSpec(memory_space=pl.ANY)],
            out_specs=pl.BlockSpec((tg, d), lambda i, nr: (i, 0)),
            scratch_shapes=[pltpu.VMEM((2, tg * pitch, V7X_LANES), F32), pltpu.SemaphoreType.DMA((2,))]),
        out_shape=jax.ShapeDtypeStruct((m, d), BF16),
        compiler_params=_cparams("arbitrary"),
        name="moe_gather",
    )(n_rows, src.reshape(m // tg, 1, tg), src.reshape(m // tg, 1, tg), h_tok)


MOE_SUB_ROWS = 128


def _expert_weights(be_ref, nxt_ref, cnt_ref, gid_ref, ng_ref, copies):
    j, i = pl.program_id(0), pl.program_id(1)
    first = jnp.logical_and(cnt_ref[i] > 0,
                            jnp.logical_or(i == 0, be_ref[i] != be_ref[jnp.maximum(i - 1, 0)]))
    slot = (j * ng_ref[0] + gid_ref[i]) % 2

    @pl.when(first)
    def _():
        @pl.when(jnp.logical_and(j == 0, i == 0))
        def _():
            for cp in copies(be_ref[0], 0, slot):
                cp.start()

        for cp in copies(be_ref[i], j, slot):
            cp.wait()
        nxt = nxt_ref[i]

        @pl.when(nxt >= 0)
        def _():
            for cp in copies(nxt, j, 1 - slot):
                cp.start()

        @pl.when(jnp.logical_and(nxt < 0, j + 1 < pl.num_programs(0)))
        def _():
            for cp in copies(be_ref[0], j + 1, 1 - slot):
                cp.start()

    return slot


def _ffn1_kernel(be_ref, nu_ref, nxt_ref, cnt_ref, gid_ref, ng_ref, x_ref, w1_hbm, bg_ref, bu_ref, o_ref,
                 wland, sem, *, ff, tf):
    def copies(e, jj, slot):
        return [pltpu.make_async_copy(w1_hbm.at[e, :, pl.ds(pl.multiple_of(half * ff + jj * tf, tf), tf)],
                                      wland.at[slot, half], sem.at[slot, half]) for half in range(2)]

    slot = _expert_weights(be_ref, nxt_ref, cnt_ref, gid_ref, ng_ref, copies)
    i = pl.program_id(1)
    for s in range(x_ref.shape[0] // MOE_SUB_ROWS):
        @pl.when(s * MOE_SUB_ROWS < cnt_ref[i])
        def _():
            rs = slice(s * MOE_SUB_ROWS, (s + 1) * MOE_SUB_ROWS)
            x = x_ref[rs, :]
            g_ = jnp.dot(x, wland[slot, 0], preferred_element_type=F32) + bg_ref[...]
            u_ = jnp.dot(x, wland[slot, 1], preferred_element_type=F32) + bu_ref[...]
            g_ = jnp.minimum(g_, SWIGLU_LIMIT)
            u_ = jnp.clip(u_, -SWIGLU_LIMIT, SWIGLU_LIMIT)
            act = (u_ + 1.0) * g_ * jax.nn.sigmoid(SWIGLU_ALPHA * g_)
            o_ref[rs, :] = act.astype(o_ref.dtype)


def _ffn2_kernel(be_ref, nu_ref, nxt_ref, cnt_ref, gid_ref, ng_ref, x_ref, w2_hbm, b_ref, o_ref,
                 wland, sem, *, tn):
    def copies(e, jj, slot):
        return [pltpu.make_async_copy(w2_hbm.at[e, :, pl.ds(pl.multiple_of(jj * tn, tn), tn)],
                                      wland.at[slot], sem.at[slot])]

    slot = _expert_weights(be_ref, nxt_ref, cnt_ref, gid_ref, ng_ref, copies)
    i = pl.program_id(1)
    for s in range(x_ref.shape[0] // MOE_SUB_ROWS):
        @pl.when(s * MOE_SUB_ROWS < cnt_ref[i])
        def _():
            rs = slice(s * MOE_SUB_ROWS, (s + 1) * MOE_SUB_ROWS)
            o_ref[rs, :] = jnp.dot(x_ref[rs, :], wland[slot], preferred_element_type=F32) + b_ref[...]


def _expert_ffn(rows, layout, w1, b1, w2, b2, *, tm, tf=512, tn=1024):
    m_pad, d = rows.shape
    n_exp, _, ff2 = w1.shape
    ff = ff2 // 2
    n_blk = m_pad // tm
    tf = _tile(ff, tf, V7X_LANES)
    tn = _tile(d, tn, V7X_LANES)
    nf = ff // tf
    assert tm % MOE_SUB_ROWS == 0

    def row_blk(i, pf):
        return jnp.minimum(i, pf[1][0] - 1)

    def expert(i, pf):
        return pf[0][i]

    act = pl.pallas_call(
        functools.partial(_ffn1_kernel, ff=ff, tf=tf),
        grid_spec=pltpu.PrefetchScalarGridSpec(
            num_scalar_prefetch=len(layout),
            grid=(nf, n_blk),
            in_specs=[pl.BlockSpec((tm, d), lambda j, i, *pf: (row_blk(i, pf), 0)),
                      pl.BlockSpec(memory_space=pl.ANY),
                      pl.BlockSpec((None, 1, tf), lambda j, i, *pf: (expert(i, pf), 0, j)),
                      pl.BlockSpec((None, 1, tf), lambda j, i, *pf: (expert(i, pf), 0, j + nf))],
            out_specs=pl.BlockSpec((tm, tf), lambda j, i, *pf: (row_blk(i, pf), j)),
            scratch_shapes=[pltpu.VMEM((2, 2, d, tf), F32), pltpu.SemaphoreType.DMA((2, 2))]),
        out_shape=jax.ShapeDtypeStruct((m_pad, ff), BF16),
        compiler_params=_cparams("arbitrary", "arbitrary"),
        name="moe_ffn1",
    )(*layout, rows, w1, b1.reshape(n_exp, 1, ff2), b1.reshape(n_exp, 1, ff2))

    return pl.pallas_call(
        functools.partial(_ffn2_kernel, tn=tn),
        grid_spec=pltpu.PrefetchScalarGridSpec(
            num_scalar_prefetch=len(layout),
            grid=(d // tn, n_blk),
            in_specs=[pl.BlockSpec((tm, ff), lambda j, i, *pf: (row_blk(i, pf), 0)),
                      pl.BlockSpec(memory_space=pl.ANY),
                      pl.BlockSpec((None, 1, tn), lambda j, i, *pf: (expert(i, pf), 0, j))],
            out_specs=pl.BlockSpec((tm, tn), lambda j, i, *pf: (row_blk(i, pf), j)),
            scratch_shapes=[pltpu.VMEM((2, ff, tn), F32), pltpu.SemaphoreType.DMA((2,))]),
        out_shape=jax.ShapeDtypeStruct((m_pad, d), F32),
        compiler_params=_cparams("arbitrary", "arbitrary"),
        name="moe_ffn2",
    )(*layout, act, w2, b2.reshape(n_exp, 1, d))


def _combine_kernel(dest_ref, ys_hbm, gate_ref, x_ref, g_ref, o_ref, buf, sem, *, tc, top_k):
    def start(r, carry):
        for j in range(top_k):
            _row_copy(ys_hbm, dest_ref[0, 0, r * top_k + j], buf.at[j], r, sem).start()
        return carry

    lax.fori_loop(0, tc, start, 0)

    def wait(r, carry):
        for j in range(top_k):
            _row_copy(ys_hbm, 0, buf.at[j], r, sem).wait()
        return carry

    lax.fori_loop(0, tc, wait, 0)
    gate = gate_ref[...]
    y = buf[0] * gate[:, 0:1]
    for j in range(1, top_k):
        y = y + buf[j] * gate[:, j:j + 1]
    x = x_ref[...] + y
    ms = jnp.mean(x * x, axis=-1, keepdims=True)
    o_ref[...] = x * lax.rsqrt(ms + RMS_EPS) * g_ref[...]


def _combine(ys, dest, gate, x, g, tc=128):
    n, d = x.shape
    tc = _tile(n, tc, V7X_SUBLANES)
    return pl.pallas_call(
        functools.partial(_combine_kernel, tc=tc, top_k=TOP_K),
        grid=(n // tc,),
        in_specs=[pl.BlockSpec((1, 1, tc * TOP_K), lambda i: (i, 0, 0), memory_space=pltpu.SMEM),
                  pl.BlockSpec(memory_space=pl.ANY),
                  pl.BlockSpec((tc, V7X_LANES), lambda i: (i, 0)),
                  pl.BlockSpec((tc, d), lambda i: (i, 0)),
                  pl.BlockSpec((1, d), lambda i: (0, 0))],
        out_specs=pl.BlockSpec((tc, d), lambda i: (i, 0)),
        out_shape=jax.ShapeDtypeStruct((n, d), F32),
        scratch_shapes=[pltpu.VMEM((TOP_K, tc, d), F32), pltpu.SemaphoreType.DMA(())],
        compiler_params=_cparams("arbitrary"),
        name="moe_combine",
    )(dest.reshape(n // tc, 1, tc * TOP_K), ys, gate, x, g.reshape(1, d))


def _moe_layout(top_i, n_exp, tm):
    n_tok = top_i.shape[0]
    m = n_tok * TOP_K
    e_flat = top_i.reshape(-1)
    onehot = (e_flat[:, None] == jnp.arange(n_exp, dtype=jnp.int32)[None, :]).astype(jnp.int32)
    rank = jnp.sum((jnp.cumsum(onehot, axis=0) - onehot) * onehot, axis=1)
    counts = jnp.sum(onehot, axis=0)
    pad_counts = (counts + tm - 1) // tm * tm
    pad_end = jnp.cumsum(pad_counts)
    pad_start = pad_end - pad_counts
    dest = pad_start[e_flat] + rank
    n_blk = (m + n_exp * (tm - 1)) // tm + 1
    tok = jnp.arange(m, dtype=jnp.int32) // TOP_K
    src = jnp.zeros((n_blk * tm,), jnp.int32).at[dest].set(tok)
    blk_start = jnp.arange(n_blk, dtype=jnp.int32) * tm
    blk_e = jnp.minimum(jnp.sum(pad_end[None, :] <= blk_start[:, None], axis=1), n_exp - 1).astype(jnp.int32)
    n_used = (pad_end[-1] // tm).astype(jnp.int32)
    blk = jnp.arange(n_blk, dtype=jnp.int32)
    used = blk < n_used
    cnt = jnp.where(used, jnp.clip(counts[blk_e] - (blk_start - pad_start[blk_e]), 0, tm), 0).astype(jnp.int32)
    starts = jnp.logical_and(used, jnp.concatenate([jnp.ones((1,), bool), blk_e[1:] != blk_e[:-1]]))
    start_idx = jnp.where(starts, blk, n_blk)
    nxt_idx = jnp.concatenate([lax.cummin(start_idx, reverse=True)[1:], jnp.full((1,), n_blk, jnp.int32)])
    nxt_e = jnp.where(nxt_idx < n_blk, blk_e[jnp.minimum(nxt_idx, n_blk - 1)], -1).astype(jnp.int32)
    gid = (jnp.cumsum(starts.astype(jnp.int32)) - 1).astype(jnp.int32)
    n_groups = jnp.sum(starts.astype(jnp.int32)).astype(jnp.int32).reshape(1)
    return dest.astype(jnp.int32), src, (blk_e, n_used.reshape(1), nxt_e, cnt, gid, n_groups)


def _pad_seq(x, t_pad):
    b, t, w = x.shape
    return x if t == t_pad else jnp.pad(x, ((0, 0), (0, t_pad - t), (0, 0)))


def kernel(x_prompt, x_sample, mem_prompt, state_gla, state_ssd, state_conv, cache_mem_k, cache_mem_v,
           norm_mix, w_in, gla_gate_w, gla_gate_b, gla_norm, ssd_conv_w, ssd_conv_b, ssd_dt_bias,
           ssd_a_log, ssd_d, ssd_norm, w_proj_gla, w_proj_ssd, w_out, norm_cross, norm_mem, w_cq,
           w_ck, w_cv, w_co, norm_ffn, router_w, router_b, moe_w1, moe_b1, moe_w2, moe_b2, norm_final):
    assert w_in.shape[0] == 1, "single layer"
    bp, tp, d = x_prompt.shape
    bs, ts, _ = x_sample.shape
    n_p, n_s = bp * tp, bs * ts
    n_all = n_p + n_s

    dv = gla_norm.shape[-1]
    val_w = w_proj_gla.shape[1]
    g_heads = val_w // dv
    key_w = gla_gate_w.shape[-1]
    rank = gla_gate_w.shape[1]
    inner = ssd_norm.shape[-1]
    s_heads = ssd_a_log.shape[-1]
    conv_dim = ssd_conv_w.shape[-1]
    n_state = state_ssd.shape[-1]
    s_groups = (conv_dim - inner) // (2 * n_state)
    kw = ssd_conv_w.shape[1]
    mem_heads, mem_hd = cache_mem_k.shape[-2:]
    mem_w = mem_heads * mem_hd
    n_exp = router_w.shape[-1]
    widths = (key_w, key_w, val_w, val_w, rank, inner, conv_dim, s_heads, d, d)
    offs = [0]
    for w_ in widths:
        offs.append(offs[-1] + w_)
    o_q, o_k, o_v, o_r, o_lr, o_z, o_xbc, o_dt, o_ga, o_gb, o_end = offs
    assert key_w == val_w // 2 and inner == d and val_w == d

    w_in_t = w_in.reshape(w_in.shape[1:]).T
    x_all = jnp.concatenate([x_prompt.reshape(n_p, d), x_sample.reshape(n_s, d)], axis=0)

    h = _rmsnorm(x_all, norm_mix[0], BF16)
    proj_a = _matmul_t(h, w_in_t, row0=0, nrows=o_lr)
    proj_b = _matmul_t(h, w_in_t, row0=o_z, nrows=o_dt - o_z)
    proj_c = _matmul_t(h, w_in_t, row0=o_ga, nrows=o_end - o_ga)
    small_wt = jnp.concatenate(
        [w_in_t[o_dt:o_ga], w_in_t[o_lr:o_z],
         jnp.zeros((V7X_LANES - s_heads - rank, d), F32)], axis=0)
    proj_s = _matmul_t(h, small_wt, row0=0, nrows=V7X_LANES)
    dt_off, lr_off = 0, s_heads

    gate_w, gate_b = gla_gate_w[0], gla_gate_b[0]
    t_pad = -(-ts // V7X_SUBLANES) * V7X_SUBLANES

    def sample_view(arr):
        return _pad_seq(arr[n_p:].reshape(bs, ts, arr.shape[1]), t_pad)

    kb = key_w
    gla_src_p = {"q": (proj_a, kb, 0), "k": (proj_a, kb, 1), "v": (proj_a, val_w, o_v // val_w),
                 "r": (proj_a, val_w, o_r // val_w), "lr": (proj_s, V7X_LANES, 0)}
    c_p = math.gcd(tp, GLA_CHUNK)
    o_p, gla_p = _gla_call(gla_src_p, gate_w, gate_b, gla_norm[0], None, heads=g_heads, batch=bp,
                           n_chunks=tp // c_p, c=c_p, t_valid=c_p, lr_off=lr_off, out_rows=n_all)
    pa_s, ps_s = sample_view(proj_a), sample_view(proj_s)
    gla_src_s = {"q": (pa_s, kb, 0), "k": (pa_s, kb, 1), "v": (pa_s, val_w, o_v // val_w),
                 "r": (pa_s, val_w, o_r // val_w), "lr": (ps_s, V7X_LANES, 0)}
    o_s, gla_s = _gla_call(gla_src_s, gate_w, gate_b, gla_norm[0], state_gla[0], heads=g_heads, batch=bs,
                           n_chunks=1, c=t_pad, t_valid=ts, lr_off=lr_off, out_rows=None)
    o_all = lax.dynamic_update_slice(o_p, o_s[:, :ts].reshape(n_s, val_w), (n_p, 0))

    bc_w = conv_dim - inner
    ssd_src_p = {"x": (proj_b, inner, 1), "bc": (proj_b, bc_w, (2 * inner) // bc_w),
                 "z": (proj_b, inner, 0), "dt": (proj_s, V7X_LANES, 0)}
    cs_p = math.gcd(tp, SSD_CHUNK)
    y_p, ssd_p, xbc_p = _ssd_call(
        ssd_src_p, ssd_conv_w[0], ssd_conv_b[0], ssd_dt_bias[0], ssd_a_log[0], ssd_d[0],
        ssd_norm[0], None, None, heads=s_heads, groups=s_groups, n=n_state, batch=bp,
        n_chunks=tp // cs_p, c=cs_p, t_valid=cs_p, dt_off=dt_off, out_rows=n_all)
    pb_s = sample_view(proj_b)
    ssd_src_s = {"x": (pb_s, inner, 1), "bc": (pb_s, bc_w, (2 * inner) // bc_w),
                 "z": (pb_s, inner, 0), "dt": (ps_s, V7X_LANES, 0)}
    y_s, ssd_s, xbc_s = _ssd_call(
        ssd_src_s, ssd_conv_w[0], ssd_conv_b[0], ssd_dt_bias[0], ssd_a_log[0], ssd_d[0],
        ssd_norm[0], state_conv[0], state_ssd[0], heads=s_heads, groups=s_groups,
        n=n_state, batch=bs, n_chunks=1, c=t_pad, t_valid=ts, dt_off=dt_off, out_rows=None)
    y_all = lax.dynamic_update_slice(y_p, y_s[:, :ts].reshape(n_s, inner), (n_p, 0))

    merged = _merge(o_all, y_all, w_proj_gla[0], w_proj_ssd[0], proj_c)
    x1 = _matmul(merged, w_out[0], res=x_all)

    h2 = _rmsnorm(x1, norm_cross[0], BF16)
    q_all = _matmul(h2, w_cq[0])
    mem = _rmsnorm(mem_prompt.reshape(-1, d), norm_mem[0], BF16)
    mk = _matmul(mem, w_ck[0])
    mv = _matmul(mem, w_cv[0])
    m_len = mem_prompt.shape[1]
    a_p = _attention(q_all[:n_p].reshape(bp, tp, mem_w), mk.reshape(bp, m_len, mem_w),
                     mv.reshape(bp, m_len, mem_w), heads=mem_heads)
    q_s = _pad_seq(q_all[n_p:].reshape(bs, ts, mem_w), t_pad)
    a_s = _attention(q_s, cache_mem_k[0].reshape(bs, -1, mem_w), cache_mem_v[0].reshape(bs, -1, mem_w),
                     heads=mem_heads)
    a_all = jnp.concatenate([a_p.reshape(n_p, mem_w), a_s[:, :ts].reshape(n_s, mem_w)], axis=0)
    x2 = _matmul(a_all, w_co[0], res=x1)

    tm_e = 512
    h3, top_pad, gate_pad = _router(x2, norm_ffn[0], router_w[0], router_b[0])
    dest, src, layout = _moe_layout(top_pad[:, :TOP_K], n_exp, tm_e)
    rows = _gather_rows(h3, src, layout[1] * tm_e, d)
    ys = _expert_ffn(rows, layout, moe_w1[0], moe_b1[0], moe_w2[0], moe_b2[0], tm=tm_e)
    y_fin = _combine(ys, dest, gate_pad, x2, norm_final)

    y_prompt = y_fin[:n_p].reshape(bp, tp, d)
    y_sample = y_fin[n_p:].reshape(bs, ts, d)
    return (y_prompt, y_sample, gla_p[None], ssd_p[None], xbc_p[None],
            mk.reshape(1, bp, m_len, mem_heads, mem_hd), mv.reshape(1, bp, m_len, mem_heads, mem_hd),
            gla_s[None], ssd_s[None], xbc_s[None])
```
